```python
import math
import jax, jax.numpy as jnp
from jax import lax
import numpy as np

D_MODEL = 1024
BATCH = 16
SEQ = 2048
DEPTH = 2

GRID_W = 64
CTX_LEN = 256
EPS = 1e-6

HEAD_DIM = 64
NA_WIDTH = D_MODEL // 2
NA_HEADS = NA_WIDTH // HEAD_DIM
NA_WIN_H = 8
NA_WIN_W = 16
S5_WIDTH = D_MODEL // 4
S5_GROUP = 16
S5_GROUPS = S5_WIDTH // S5_GROUP
S5_STATE = 64
S5_DT_MIN = 0.001
S5_DT_MAX = 0.1
FNET_WIDTH = D_MODEL // 4
FNET_HEAD_DIM = 64
FNET_HEADS = FNET_WIDTH // FNET_HEAD_DIM
MIX_WIDTH = NA_WIDTH + S5_WIDTH + FNET_WIDTH
IN_COLS = 3 * NA_WIDTH + S5_WIDTH + FNET_WIDTH
IN_SPLITS = (NA_WIDTH, 2 * NA_WIDTH, 3 * NA_WIDTH, 3 * NA_WIDTH + S5_WIDTH)
ATTN_SCALE = HEAD_DIM ** -0.5
ROPE_BASE = 100.0
ROPE_PAIRS_PER_AXIS = HEAD_DIM // 4

N_EXPERTS = 16
N_EXPERT_GROUPS = 4
EXPERTS_PER_GROUP = N_EXPERTS // N_EXPERT_GROUPS
TOP_K = 2
EXPERT_FF = 512

kernel_name = "hybrid_na_s5_fnet_moe_dit"


def rms_norm(x, g):
    xf = x.astype(jnp.float32)
    y = xf * lax.rsqrt(jnp.mean(xf * xf, axis=-1, keepdims=True) + EPS)
    return (y * g.astype(jnp.float32)).astype(x.dtype)


def axial_rope(n_tokens):
    t = jnp.arange(n_tokens, dtype=jnp.int32)
    row = (t // GRID_W).astype(jnp.float32)
    col = (t % GRID_W).astype(jnp.float32)
    inv_freq = ROPE_BASE ** (-jnp.arange(ROPE_PAIRS_PER_AXIS, dtype=jnp.float32) / ROPE_PAIRS_PER_AXIS)
    ang = jnp.concatenate([row[:, None] * inv_freq, col[:, None] * inv_freq], axis=-1)
    return jnp.cos(ang), jnp.sin(ang)


def apply_rope(x, cos, sin):
    half = HEAD_DIM // 2
    x1, x2 = x[..., :half], x[..., half:]
    cs = cos[:, None, :].astype(x.dtype)
    sn = sin[:, None, :].astype(x.dtype)
    return jnp.concatenate([x1 * cs - x2 * sn, x1 * sn + x2 * cs], axis=-1)


def neighbourhood_attention(q, k, v, k_ctx, v_ctx, rpb):
    B, L, H, dh = q.shape
    rows = L // GRID_W
    kh = min(NA_WIN_H, rows)
    kw = NA_WIN_W
    qg = jnp.moveaxis(q.reshape(B, rows, GRID_W, H, dh), 1, 0)
    kg = k.reshape(B, rows, GRID_W, H, dh)
    vg = v.reshape(B, rows, GRID_W, H, dh)
    col = jnp.arange(GRID_W, dtype=jnp.int32)
    col_start = jnp.clip(col - kw // 2, 0, GRID_W - kw)
    col_idx = col_start[:, None] + jnp.arange(kw, dtype=jnp.int32)[None, :]
    col_off = col_idx - col[:, None] + (NA_WIN_W - 1)

    def row_block(args):
        r, q_r = args
        row_start = jnp.clip(r - kh // 2, 0, rows - kh)
        k_rows = lax.dynamic_slice_in_dim(kg, row_start, kh, axis=1)
        v_rows = lax.dynamic_slice_in_dim(vg, row_start, kh, axis=1)
        k_win = k_rows[:, :, col_idx]
        v_win = v_rows[:, :, col_idx]
        row_off = row_start + jnp.arange(kh, dtype=jnp.int32) - r + (NA_WIN_H - 1)
        bias = rpb[:, row_off][:, :, col_off]
        bias = jnp.transpose(bias, (0, 2, 1, 3)).reshape(H, GRID_W, kh * kw)
        s_loc = jnp.einsum('bchd,bicjhd->bhcij', q_r, k_win).reshape(B, H, GRID_W, kh * kw)
        s_loc = s_loc.astype(jnp.float32) * ATTN_SCALE + bias.astype(jnp.float32)
        s_ctx = jnp.einsum('bchd,bjhd->bhcj', q_r, k_ctx).astype(jnp.float32) * ATTN_SCALE
        p = jax.nn.softmax(jnp.concatenate([s_loc, s_ctx], axis=-1), axis=-1).astype(v.dtype)
        p_loc = p[..., :kh * kw].reshape(B, H, GRID_W, kh, kw)
        p_ctx = p[..., kh * kw:]
        return (jnp.einsum('bhcij,bicjhd->bchd', p_loc, v_win)
                + jnp.einsum('bhcj,bjhd->bchd', p_ctx, v_ctx))

    out = lax.map(row_block, (jnp.arange(rows, dtype=jnp.int32), qg))
    return jnp.moveaxis(out, 0, 1).reshape(B, L, H, dh)


def context_attention(qc, kc, vc):
    s = jnp.einsum('bqhd,bkhd->bhqk', qc, kc).astype(jnp.float32) * ATTN_SCALE
    p = jax.nn.softmax(s, axis=-1).astype(vc.dtype)
    return jnp.einsum('bhqk,bkhd->bqhd', p, vc)


def s5_discretise(lam_re, lam_im, log_step, b_re, b_im):
    lam = lax.complex(lam_re.astype(jnp.float32), lam_im.astype(jnp.float32))
    step = jnp.exp(log_step.astype(jnp.float32))[:, None]
    lam_bar = jnp.exp(lam * step)
    b = lax.complex(b_re.astype(jnp.float32), b_im.astype(jnp.float32))
    b_bar = ((lam_bar - 1.0) / lam)[..., None] * b
    return lam_bar, b_bar


def _linear_recurrence(left, right):
    a_l, b_l = left
    a_r, b_r = right
    return a_l * a_r, a_r * b_l + b_r


def s5_scan(u, lam_bar, b_bar, h0, reverse):
    bu = jnp.einsum('blgc,gpc->blgp', u.astype(jnp.complex64), b_bar)
    if reverse:
        bu = jnp.flip(bu, axis=1)
    if h0 is not None:
        bu = bu.at[:, 0].add(lam_bar * h0)
    a = jnp.broadcast_to(lam_bar, (1, bu.shape[1]) + lam_bar.shape)
    _, h = lax.associative_scan(_linear_recurrence, (a, bu), axis=1)
    return jnp.flip(h, axis=1) if reverse else h


def s5_glu(y, w_glu, b_glu):
    g = jax.nn.gelu(y)
    return g * jax.nn.sigmoid(g @ w_glu + b_glu)


def s5_mixer(u, u_ctx, lam_re, lam_im, log_step, b_re, b_im, c_re, c_im, d_skip, w_glu, b_glu, ctx_out):
    B, L, _ = u.shape
    Lc = u_ctx.shape[1]
    ug = u.astype(jnp.float32).reshape(B, L, S5_GROUPS, S5_GROUP)
    ucg = u_ctx.astype(jnp.float32).reshape(B, Lc, S5_GROUPS, S5_GROUP)
    d = d_skip.astype(jnp.float32).reshape(S5_GROUPS, S5_GROUP)
    y = d * ug
    yc = d * ucg if ctx_out else None
    for direction in range(2):
        rev = direction == 1
        lam_bar, b_bar = s5_discretise(lam_re[direction], lam_im[direction], log_step[direction],
                                       b_re[direction], b_im[direction])
        c_mat = lax.complex(c_re[direction].astype(jnp.float32), c_im[direction].astype(jnp.float32))
        h_ctx = s5_scan(ucg, lam_bar, b_bar, None, rev)
        h0 = h_ctx[:, 0] if rev else h_ctx[:, -1]
        h = s5_scan(ug, lam_bar, b_bar, h0, rev)
        y = y + jnp.einsum('blgp,gcp->blgc', h, c_mat).real
        if ctx_out:
            yc = yc + jnp.einsum('blgp,gcp->blgc', h_ctx, c_mat).real
    out = s5_glu(y.reshape(B, L, S5_WIDTH).astype(u.dtype), w_glu, b_glu)
    if not ctx_out:
        return out, None
    out_c = s5_glu(yc.reshape(B, Lc, S5_WIDTH).astype(u.dtype), w_glu, b_glu)
    return out, out_c


def fourier_mixer(u, w_fnet):
    f = jnp.fft.fft2(u.astype(jnp.float32), axes=(1, 3), norm='ortho').real.astype(u.dtype)
    return jnp.einsum('blhd,hde->blhe', f, w_fnet)


def token_mixers(h, hc, rope_cos, rope_sin, w_in, q_norm_g, k_norm_g, rpb,
                 lam_re, lam_im, log_step, b_re, b_im, c_re, c_im, d_skip,
                 w_glu, b_glu, w_fnet, w_out, ctx_out):
    B, L, _ = h.shape
    Lc = hc.shape[1]
    q, k, v, u_s5, u_f = jnp.split(h @ w_in, IN_SPLITS, axis=-1)
    q = apply_rope(rms_norm(q.reshape(B, L, NA_HEADS, HEAD_DIM), q_norm_g), rope_cos, rope_sin)
    k = apply_rope(rms_norm(k.reshape(B, L, NA_HEADS, HEAD_DIM), k_norm_g), rope_cos, rope_sin)
    v = v.reshape(B, L, NA_HEADS, HEAD_DIM)
    if ctx_out:
        qc, kc, vc, uc_s5, uc_f = jnp.split(hc @ w_in, IN_SPLITS, axis=-1)
    else:
        kc, vc, uc_s5 = jnp.split(hc @ w_in[:, NA_WIDTH:3 * NA_WIDTH + S5_WIDTH],
                                  (NA_WIDTH, 2 * NA_WIDTH), axis=-1)
    kc = rms_norm(kc.reshape(B, Lc, NA_HEADS, HEAD_DIM), k_norm_g)
    vc = vc.reshape(B, Lc, NA_HEADS, HEAD_DIM)

    att = neighbourhood_attention(q, k, v, kc, vc, rpb).reshape(B, L, NA_WIDTH)
    ssm, ssm_c = s5_mixer(u_s5, uc_s5, lam_re, lam_im, log_step, b_re, b_im, c_re, c_im, d_skip,
                          w_glu, b_glu, ctx_out)
    fou = fourier_mixer(u_f.reshape(B, L, FNET_HEADS, FNET_HEAD_DIM), w_fnet).reshape(B, L, FNET_WIDTH)
    out = jnp.concatenate([att, ssm, fou], axis=-1) @ w_out
    if not ctx_out:
        return out, None
    qc = rms_norm(qc.reshape(B, Lc, NA_HEADS, HEAD_DIM), q_norm_g)
    att_c = context_attention(qc, kc, vc).reshape(B, Lc, NA_WIDTH)
    fou_c = fourier_mixer(uc_f.reshape(B, Lc, FNET_HEADS, FNET_HEAD_DIM), w_fnet).reshape(B, Lc, FNET_WIDTH)
    out_c = jnp.concatenate([att_c, ssm_c, fou_c], axis=-1) @ w_out
    return out, out_c


def moe_ffn(t, w_router, b_router, w_gate, w_up, w_down):
    n_tok = t.shape[0]
    affinity = jax.nn.sigmoid((t @ w_router).astype(jnp.float32))
    select = (affinity + b_router.astype(jnp.float32)).reshape(n_tok, N_EXPERT_GROUPS, EXPERTS_PER_GROUP)
    group_score = jnp.sum(lax.top_k(select, TOP_K)[0], axis=-1)
    g_idx = jnp.argmax(group_score, axis=-1)
    in_group = jnp.take_along_axis(select, g_idx[:, None, None], axis=1)[:, 0]
    e_idx = g_idx[:, None] * EXPERTS_PER_GROUP + lax.top_k(in_group, TOP_K)[1]
    w = jnp.take_along_axis(affinity, e_idx, axis=-1)
    w = w / jnp.sum(w, axis=-1, keepdims=True)
    gates = jnp.sum(jax.nn.one_hot(e_idx, N_EXPERTS, dtype=jnp.float32) * w[..., None], axis=1).astype(t.dtype)
    y = jnp.zeros_like(t)
    for e in range(N_EXPERTS):
        he = jax.nn.silu(t @ w_gate[e]) * (t @ w_up[e])
        y = y + gates[:, e:e + 1] * (he @ w_down[e])
    return y


def setup_inputs(seed: int = 0) -> dict:
    key = jax.random.key(seed)
    ks = jax.random.split(key, 32)
    f32 = jnp.float32
    D = D_MODEL

    def nrm(k, shape, scale):
        return jax.random.normal(k, shape, f32) * scale

    s5_shape = (DEPTH, 2, S5_GROUPS, S5_STATE)
    return {
        "x": nrm(ks[0], (BATCH, SEQ, D), 1.0),
        "c": nrm(ks[1], (BATCH, D), 1.0),
        "ctx": nrm(ks[2], (BATCH, CTX_LEN, D), 1.0),
        "c_ctx": nrm(ks[3], (D,), 1.0),
        "w_ada": nrm(ks[4], (DEPTH, D, 6 * D), 0.5 * D ** -0.5),
        "b_ada": nrm(ks[5], (DEPTH, 6 * D), 0.01),
        "norm1_g": 1.0 + nrm(ks[6], (DEPTH, D), 0.01),
        "norm2_g": 1.0 + nrm(ks[7], (DEPTH, D), 0.01),
        "w_in": nrm(ks[8], (DEPTH, D, IN_COLS), D ** -0.5),
        "q_norm_g": 1.0 + nrm(ks[9], (DEPTH, HEAD_DIM), 0.01),
        "k_norm_g": 1.0 + nrm(ks[10], (DEPTH, HEAD_DIM), 0.01),
        "rpb": nrm(ks[11], (DEPTH, NA_HEADS, 2 * NA_WIN_H - 1, 2 * NA_WIN_W - 1), 0.02),
        "s5_lam_re": -0.5 + nrm(ks[12], s5_shape, 0.01),
        "s5_lam_im": math.pi * jnp.arange(S5_STATE, dtype=f32) + nrm(ks[13], s5_shape, 0.01),
        "s5_log_step": jax.random.uniform(ks[14], (DEPTH, 2, S5_GROUPS), f32,
                                          math.log(S5_DT_MIN), math.log(S5_DT_MAX)),
        "s5_b_re": nrm(ks[15], s5_shape + (S5_GROUP,), (2 * S5_GROUP) ** -0.5),
        "s5_b_im": nrm(ks[16], s5_shape + (S5_GROUP,), (2 * S5_GROUP) ** -0.5),
        "s5_c_re": nrm(ks[17], (DEPTH, 2, S5_GROUPS, S5_GROUP, S5_STATE), (2 * S5_STATE) ** -0.5),
        "s5_c_im": nrm(ks[18], (DEPTH, 2, S5_GROUPS, S5_GROUP, S5_STATE), (2 * S5_STATE) ** -0.5),
        "s5_d": nrm(ks[19], (DEPTH, S5_WIDTH), 1.0),
        "w_glu": nrm(ks[20], (DEPTH, S5_WIDTH, S5_WIDTH), S5_WIDTH ** -0.5),
        "b_glu": nrm(ks[21], (DEPTH, S5_WIDTH), 0.01),
        "w_fnet": nrm(ks[22], (DEPTH, FNET_HEADS, FNET_HEAD_DIM, FNET_HEAD_DIM), FNET_HEAD_DIM ** -0.5),
        "w_out": nrm(ks[23], (DEPTH, MIX_WIDTH, D), MIX_WIDTH ** -0.5),
        "w_router": nrm(ks[24], (D, N_EXPERTS), D ** -0.5),
        "b_router": nrm(ks[25], (N_EXPERTS,), 0.01),
        "w_gate": nrm(ks[26], (DEPTH, N_EXPERTS, D, EXPERT_FF), D ** -0.5),
        "w_up": nrm(ks[27], (DEPTH, N_EXPERTS, D, EXPERT_FF), D ** -0.5),
        "w_down": nrm(ks[28], (DEPTH, N_EXPERTS, EXPERT_FF, D), EXPERT_FF ** -0.5),
    }


def reference(x, c, ctx, c_ctx, w_ada, b_ada, norm1_g, norm2_g, w_in, q_norm_g, k_norm_g, rpb,
              s5_lam_re, s5_lam_im, s5_log_step, s5_b_re, s5_b_im, s5_c_re, s5_c_im, s5_d,
              w_glu, b_glu, w_fnet, w_out, w_router, b_router, w_gate, w_up, w_down):
    B, L, D = x.shape
    rope_cos, rope_sin = axial_rope(L)
    silu_c = jax.nn.silu(c)
    silu_cc = jax.nn.silu(c_ctx)
    xc = ctx
    for l in range(DEPTH):
        ctx_out = l < DEPTH - 1
        mod = jnp.split((silu_c @ w_ada[l] + b_ada[l])[:, None, :], 6, axis=-1)
        mod_c = jnp.split(silu_cc @ w_ada[l] + b_ada[l], 6, axis=-1)
        h = rms_norm(x, norm1_g[l]) * (1 + mod[1]) + mod[0]
        hc = rms_norm(xc, norm1_g[l]) * (1 + mod_c[1]) + mod_c[0]
        o, oc = token_mixers(h, hc, rope_cos, rope_sin, w_in[l], q_norm_g[l], k_norm_g[l], rpb[l],
                             s5_lam_re[l], s5_lam_im[l], s5_log_step[l], s5_b_re[l], s5_b_im[l],
                             s5_c_re[l], s5_c_im[l], s5_d[l], w_glu[l], b_glu[l], w_fnet[l], w_out[l],
                             ctx_out)
        x = x + mod[2] * o
        h2 = rms_norm(x, norm2_g[l]) * (1 + mod[4]) + mod[3]
        if ctx_out:
            xc = xc + mod_c[2] * oc
            h2c = rms_norm(xc, norm2_g[l]) * (1 + mod_c[4]) + mod_c[3]
            tokens = jnp.concatenate([h2.reshape(-1, D), h2c.reshape(-1, D)], axis=0)
            y = moe_ffn(tokens, w_router, b_router, w_gate[l], w_up[l], w_down[l])
            x = x + mod[5] * y[:B * L].reshape(B, L, D)
            xc = xc + mod_c[5] * y[B * L:].reshape(B, -1, D)
        else:
            y = moe_ffn(h2.reshape(-1, D), w_router, b_router, w_gate[l], w_up[l], w_down[l])
            x = x + mod[5] * y.reshape(B, L, D)
    return x
```

```python
import functools
import math

import numpy as np
import jax
import jax.numpy as jnp
from jax import lax
from jax.experimental import pallas as pl
from jax.experimental.pallas import tpu as pltpu

F32 = jnp.float32
BF16 = jnp.bfloat16

GRID_W = 64
EPS = 1e-6
HEAD_DIM = 64
NA_WIDTH = 512
NA_HEADS = 8
NA_WIN_H = 8
NA_WIN_W = 16
S5_WIDTH = 256
S5_GROUP = 16
S5_GROUPS = 16
S5_STATE = 64
S5_COLS = S5_GROUPS * S5_STATE
FNET_WIDTH = 256
FNET_HEADS = 4
FNET_HEAD_DIM = 64
IN_COLS = 3 * NA_WIDTH + S5_WIDTH + FNET_WIDTH
ATTN_SCALE = HEAD_DIM ** -0.5
ROPE_BASE = 100.0
ROPE_PAIRS_PER_AXIS = HEAD_DIM // 4
N_EXPERTS = 16
N_EXPERT_GROUPS = 4
EXPERTS_PER_GROUP = 4
EXPERT_FF = 512

LANES = 128
SUBLANES = 8
VMEM_LIMIT = 56 * 1024 * 1024

NEG_BIG = -1e30
NT_DIMS = (((1,), (1,)), ((), ()))


def _split_bf16(x):
    hi = x.astype(BF16)
    lo = (x - hi.astype(F32)).astype(BF16)
    return hi, lo


def _dot_f32(a, b, dims=None):
    a_hi, a_lo = _split_bf16(a)
    b_hi, b_lo = _split_bf16(b)
    if dims is None:
        dot = functools.partial(jnp.dot, preferred_element_type=F32)
    else:
        dot = functools.partial(lax.dot_general, dimension_numbers=dims, preferred_element_type=F32)
    return dot(a_hi, b_hi) + dot(a_hi, b_lo) + dot(a_lo, b_hi)


def _params(*sem):
    return pltpu.CompilerParams(dimension_semantics=sem, vmem_limit_bytes=VMEM_LIMIT)


def _mod_kernel(c_ref, w_ref, b_ref, o_ref):
    c = c_ref[...]
    s = c * jax.nn.sigmoid(c)
    o_ref[0] = _dot_f32(s, w_ref[0]) + b_ref[0]


def _modulation(cvec, w_ada, b_ada):
    depth, d, n = w_ada.shape
    r = cvec.shape[0]
    tn = 768
    return pl.pallas_call(
        _mod_kernel,
        out_shape=jax.ShapeDtypeStruct((depth, r, n), F32),
        grid=(depth, n // tn),
        in_specs=[pl.BlockSpec((r, d), lambda l, j: (0, 0)),
                  pl.BlockSpec((1, d, tn), lambda l, j: (l, 0, j)),
                  pl.BlockSpec((1, 1, tn), lambda l, j: (l, 0, j))],
        out_specs=pl.BlockSpec((1, r, tn), lambda l, j: (l, 0, j)),
        compiler_params=_params("arbitrary", "arbitrary"),
    )(cvec, w_ada, b_ada.reshape(depth, 1, n))


def _inproj_kernel(x_ref, mod_ref, g1_ref, w_ref, qg_ref, kg_ref, gm_ref, cos_ref, sa_ref, sb_ref,
                   q_ref, k_ref, v_ref, us_ref, uf_ref):
    x = x_ref[0]
    mod = mod_ref[0, 0]
    r = lax.rsqrt(jnp.mean(x * x, axis=-1, keepdims=True) + EPS)
    h = (x * r * g1_ref[...]) * (1.0 + mod[1:2]) + mod[0:1]
    proj = jnp.dot(h.astype(BF16), w_ref[...], preferred_element_type=F32)

    cos = cos_ref[...]
    sin_lo = sa_ref[...]
    sin_hi = sb_ref[...]
    gm = gm_ref[...]
    nw = NA_WIDTH

    def head_norm_rope(t, g):
        ss = jnp.dot((t * t).astype(BF16), gm, preferred_element_type=F32)
        tn = t * lax.rsqrt(ss * (1.0 / HEAD_DIM) + EPS) * g
        half = HEAD_DIM // 2
        return (tn * cos + pltpu.roll(tn, nw - half, 1) * sin_lo + pltpu.roll(tn, half, 1) * sin_hi)

    q_ref[0] = head_norm_rope(proj[:, 0:nw], qg_ref[...]).astype(BF16)
    k_ref[0] = head_norm_rope(proj[:, nw:2 * nw], kg_ref[...]).astype(BF16)
    v_ref[0] = proj[:, 2 * nw:3 * nw].astype(BF16)
    us_ref[...] = proj[:, 3 * nw:3 * nw + S5_WIDTH]
    uf_ref[0] = proj[:, 3 * nw + S5_WIDTH:].astype(BF16)


def _in_projection(xa, modtab, g1, w_in_bf, qg, kg, gm, cos, sin_lo, sin_hi, tm, nct):
    b, s, d = xa.shape
    nt = s // tm
    nw = NA_WIDTH
    row = lambda i, bb: (bb, i, 0)
    const2 = lambda i, bb: (0, 0)
    tab = lambda i, bb: (i, 0)
    return pl.pallas_call(
        _inproj_kernel,
        out_shape=(jax.ShapeDtypeStruct((b, s, nw), BF16),
                   jax.ShapeDtypeStruct((b, s, nw), BF16),
                   jax.ShapeDtypeStruct((b, s, nw), BF16),
                   jax.ShapeDtypeStruct((s, b * S5_WIDTH), F32),
                   jax.ShapeDtypeStruct((b, s, FNET_WIDTH), BF16)),
        grid=(nt, b),
        in_specs=[pl.BlockSpec((1, tm, d), row),
                  pl.BlockSpec((1, 1, 6, d), lambda i, bb: (bb, jnp.where(i >= nct, 1, 0), 0, 0)),
                  pl.BlockSpec((1, d), const2),
                  pl.BlockSpec((d, IN_COLS), const2),
                  pl.BlockSpec((1, nw), const2),
                  pl.BlockSpec((1, nw), const2),
                  pl.BlockSpec((nw, nw), const2),
                  pl.BlockSpec((tm, nw), tab),
                  pl.BlockSpec((tm, nw), tab),
                  pl.BlockSpec((tm, nw), tab)],
        out_specs=(pl.BlockSpec((1, tm, nw), row),
                   pl.BlockSpec((1, tm, nw), row),
                   pl.BlockSpec((1, tm, nw), row),
                   pl.BlockSpec((tm, S5_WIDTH), lambda i, bb: (i, bb)),
                   pl.BlockSpec((1, tm, FNET_WIDTH), row)),
        compiler_params=_params("arbitrary", "arbitrary"),
    )(xa, modtab, g1, w_in_bf, qg, kg, gm, cos, sin_lo, sin_hi)


def _attn_blocks(rows, rb):
    kh = min(NA_WIN_H, rows)
    kr = min(rb + kh - 1, rows)
    blocks, classes = [], {}
    for r0 in range(0, rows, rb):
        ks = min(max(r0 - kh // 2, 0), rows - kr)
        starts = tuple(min(max(r - kh // 2, 0), rows - kh) - ks for r in range(r0, r0 + rb))
        key = (r0 - ks, starts)
        cls = classes.setdefault(key, len(classes))
        blocks.append((r0, ks, cls))
    return kh, kr, blocks, list(classes.keys())


def _attn_bias_tables(rpb, rows, rb):
    kh, kr, blocks, class_keys = _attn_blocks(rows, rb)
    w = GRID_W
    col = np.arange(w)
    col_start = np.clip(col - NA_WIN_W // 2, 0, w - NA_WIN_W)
    tabs = []
    for q_minus_ks, starts in class_keys:
        rq = np.arange(rb)[:, None, None, None]
        c = col[None, :, None, None]
        i = np.arange(kr)[None, None, :, None]
        j = col[None, None, None, :]
        rs = np.asarray(starts)[:, None, None, None]
        cs = col_start[None, :, None, None]
        valid = (i >= rs) & (i < rs + kh) & (j >= cs) & (j < cs + NA_WIN_W)
        row_off = np.clip(i - (q_minus_ks + rq) + (NA_WIN_H - 1), 0, 2 * NA_WIN_H - 2)
        col_off = np.clip(j - c + (NA_WIN_W - 1), 0, 2 * NA_WIN_W - 2)
        shape = (rb, w, kr, w)
        valid = np.broadcast_to(valid, shape).reshape(rb * w, kr * w)
        row_off = np.broadcast_to(row_off, shape).reshape(rb * w, kr * w)
        col_off = np.broadcast_to(col_off, shape).reshape(rb * w, kr * w)
        bias = rpb.astype(F32)[:, row_off, col_off]
        tabs.append(jnp.where(valid[None], bias, NEG_BIG))
    return jnp.stack(tabs, axis=0)


def _attn_kernel(q_ref, k_ref, v_ref, bias_ref, o_ref, *, lc, rb, kr, blocks):
    lane = lax.broadcasted_iota(jnp.int32, (1, LANES), 1)
    first_head = lane < HEAD_DIM

    def stack_heads(qb):
        z = jnp.zeros_like(qb)
        return jnp.concatenate([jnp.where(first_head, qb, z), jnp.where(first_head, z, qb)], axis=0)

    def unstack_heads(o, n):
        return jnp.where(first_head, o[:n], o[n:])

    kc = k_ref[0, 0:lc, :]
    vc = v_ref[0, 0:lc, :]

    s = lax.dot_general(stack_heads(q_ref[0, 0:lc, :]), kc, NT_DIMS, preferred_element_type=F32)
    m = jnp.max(s, axis=-1, keepdims=True)
    p = jnp.exp(s - m)
    den = jnp.sum(p, axis=-1, keepdims=True)
    o = jnp.dot(p.astype(BF16), vc, preferred_element_type=F32) / den
    o_ref[0, 0:lc, :] = unstack_heads(o, lc).astype(BF16)

    nq = rb * GRID_W
    nk = kr * GRID_W
    for r0, ks, cls in blocks:
        q0 = lc + r0 * GRID_W
        k0 = lc + ks * GRID_W
        q2 = stack_heads(q_ref[0, q0:q0 + nq, :])
        k_loc = k_ref[0, k0:k0 + nk, :]
        v_loc = v_ref[0, k0:k0 + nk, :]
        s_loc = lax.dot_general(q2, k_loc, NT_DIMS, preferred_element_type=F32)
        s_loc = s_loc + bias_ref[cls].reshape(2 * nq, nk)
        s_ctx = lax.dot_general(q2, kc, NT_DIMS, preferred_element_type=F32)
        m = jnp.maximum(jnp.max(s_loc, axis=-1, keepdims=True), jnp.max(s_ctx, axis=-1, keepdims=True))
        p_loc = jnp.exp(s_loc - m)
        p_ctx = jnp.exp(s_ctx - m)
        den = jnp.sum(p_loc, axis=-1, keepdims=True) + jnp.sum(p_ctx, axis=-1, keepdims=True)
        o = (jnp.dot(p_loc.astype(BF16), v_loc, preferred_element_type=F32)
             + jnp.dot(p_ctx.astype(BF16), vc, preferred_element_type=F32)) / den
        o_ref[0, q0:q0 + nq, :] = unstack_heads(o, nq).astype(BF16)


def _attention(q, k, v, bias, lc, rows, rb):
    b, s, nw = q.shape
    _, kr, blocks, _ = _attn_blocks(rows, rb)
    ncls = bias.shape[0]
    spec = pl.BlockSpec((1, s, LANES), lambda bb, p: (bb, 0, p))
    return pl.pallas_call(
        functools.partial(_attn_kernel, lc=lc, rb=rb, kr=kr, blocks=blocks),
        out_shape=jax.ShapeDtypeStruct((b, s, nw), BF16),
        grid=(b, nw // LANES),
        in_specs=[spec, spec, spec,
                  pl.BlockSpec((ncls, 2, rb * GRID_W, kr * GRID_W), lambda bb, p: (0, p, 0, 0))],
        out_specs=spec,
        compiler_params=_params("arbitrary", "arbitrary"),
    )(q, k, v, bias)


def _s5_matrices(lam_re, lam_im, log_step, b_re, b_im, c_re, c_im, batch):
    lam_re = lam_re.astype(F32)
    lam_im = lam_im.astype(F32)
    step = jnp.exp(log_step.astype(F32))[..., None]
    mag = jnp.exp(lam_re * step)
    a_re = mag * jnp.cos(lam_im * step)
    a_im = mag * jnp.sin(lam_im * step)
    den = lam_re * lam_re + lam_im * lam_im
    co_re = ((a_re - 1.0) * lam_re + a_im * lam_im) / den
    co_im = (a_im * lam_re - (a_re - 1.0) * lam_im) / den
    b_re = b_re.astype(F32)
    b_im = b_im.astype(F32)
    bb_re = co_re[..., None] * b_re - co_im[..., None] * b_im
    bb_im = co_re[..., None] * b_im + co_im[..., None] * b_re
    eye = jnp.eye(S5_GROUPS, dtype=F32)
    to_in = lambda t: jnp.einsum('dgpc,gh->dgchp', t, eye).reshape(2, S5_WIDTH, S5_COLS)
    bmat = jnp.concatenate([to_in(bb_re), to_in(bb_im)], axis=-1)
    to_out = lambda t: jnp.einsum('dgcp,gh->dgphc', t.astype(F32), eye).reshape(2, S5_COLS, S5_WIDTH)
    cmat = jnp.concatenate([to_out(c_re), -to_out(c_im)], axis=1)
    bc = lambda t: jnp.broadcast_to(t.reshape(2, 1, S5_COLS), (2, batch, S5_COLS))
    return bmat.astype(BF16), cmat.astype(BF16), bc(a_re), bc(a_im)


def _s5_kernel(u_ref, bm_ref, cm_ref, ar_ref, ai_ref, y_ref, hbuf, st_re, st_im, *, tc, batch):
    d = pl.program_id(0)
    j = pl.program_id(1)

    @pl.when(j == 0)
    def _():
        st_re[...] = jnp.zeros_like(st_re)
        st_im[...] = jnp.zeros_like(st_im)

    hbuf[...] = jnp.dot(u_ref[...].astype(BF16), bm_ref[0], preferred_element_type=F32)

    ncol = 256
    for cg in range(S5_COLS // ncol):
        re = slice(cg * ncol, (cg + 1) * ncol)
        im = slice(S5_COLS + cg * ncol, S5_COLS + (cg + 1) * ncol)
        a_re = ar_ref[0, :, re]
        a_im = ai_ref[0, :, re]

        def step(tt, carry, re=re, im=im, a_re=a_re, a_im=a_im):
            h_re, h_im = carry
            t = jnp.where(d == 0, tt, tc - 1 - tt)
            row = pl.multiple_of(t * batch, batch)
            n_re = a_re * h_re - a_im * h_im + hbuf[pl.ds(row, batch), re]
            n_im = a_re * h_im + a_im * h_re + hbuf[pl.ds(row, batch), im]
            hbuf[pl.ds(row, batch), re] = n_re
            hbuf[pl.ds(row, batch), im] = n_im
            return n_re, n_im

        h_re, h_im = lax.fori_loop(0, tc, step, (st_re[:, re], st_im[:, re]), unroll=4)
        st_re[:, re] = h_re
        st_im[:, re] = h_im

    y_ref[0] = jnp.dot(hbuf[...].astype(BF16), cm_ref[0], preferred_element_type=F32)


def _s5_scan(us2, bmat, cmat, a_re, a_im, batch, lc, tc):
    n, w = us2.shape
    s = n // batch
    nch = s // tc
    nch_c = lc // tc

    def pos(d, j):
        back = jnp.where(j < nch_c, nch_c - 1 - j, nch + nch_c - 1 - j)
        return jnp.where(d == 0, j, back)

    per_dir = lambda d, j: (d, 0, 0)
    return pl.pallas_call(
        functools.partial(_s5_kernel, tc=tc, batch=batch),
        out_shape=jax.ShapeDtypeStruct((2, n, w), F32),
        grid=(2, nch),
        in_specs=[pl.BlockSpec((tc * batch, w), lambda d, j: (pos(d, j), 0)),
                  pl.BlockSpec((1, w, 2 * S5_COLS), per_dir),
                  pl.BlockSpec((1, 2 * S5_COLS, w), per_dir),
                  pl.BlockSpec((1, batch, S5_COLS), per_dir),
                  pl.BlockSpec((1, batch, S5_COLS), per_dir)],
        out_specs=pl.BlockSpec((1, tc * batch, w), lambda d, j: (d, pos(d, j), 0)),
        scratch_shapes=[pltpu.VMEM((tc * batch, 2 * S5_COLS), F32),
                        pltpu.VMEM((batch, S5_COLS), F32),
                        pltpu.VMEM((batch, S5_COLS), F32)],
        compiler_params=_params("arbitrary", "arbitrary"),
    )(us2, bmat, cmat, a_re, a_im)


def _dft_cos_sin(n):
    k = np.arange(n, dtype=np.int64)
    ang = 2.0 * np.pi * ((k[:, None] * k[None, :]) % n).astype(np.float64) / n
    return np.cos(ang), np.sin(ang)


def _fnet_constants(l, lc):
    def seq_matrix(n):
        c, s = _dft_cos_sin(n)
        scale = 1.0 / math.sqrt(n * FNET_HEAD_DIM)
        return jnp.asarray(np.concatenate([c, -s], axis=1) * scale, dtype=F32).astype(BF16)
    c64, s64 = _dft_cos_sin(FNET_HEAD_DIM)
    eye = np.eye(FNET_HEADS)
    return (seq_matrix(l), seq_matrix(lc),
            jnp.asarray(np.kron(eye, c64), dtype=F32).astype(BF16),
            jnp.asarray(np.kron(eye, s64), dtype=F32).astype(BF16))


def _fnet_kernel(u_ref, csl_ref, csc_ref, c64_ref, s64_ref, w_ref, o_ref, tl_ref, tc_ref, *, lc, l):
    i = pl.program_id(1)
    w = w_ref[...]

    @pl.when(i == 0)
    def _():
        u = u_ref[0]
        t_cos = jnp.dot(u, c64_ref[...], preferred_element_type=F32).astype(BF16)
        t_sin = jnp.dot(u, s64_ref[...], preferred_element_type=F32).astype(BF16)
        tc_ref[0:lc, :] = t_cos[0:lc]
        tc_ref[lc:2 * lc, :] = t_sin[0:lc]
        tl_ref[0:l, :] = t_cos[lc:]
        tl_ref[l:2 * l, :] = t_sin[lc:]
        z = jnp.dot(csc_ref[...], tc_ref[...], preferred_element_type=F32)
        o_ref[0] = jnp.dot(z.astype(BF16), w, preferred_element_type=F32).astype(BF16)

    @pl.when(i > 0)
    def _():
        z = jnp.dot(csl_ref[...], tl_ref[...], preferred_element_type=F32)
        o_ref[0] = jnp.dot(z.astype(BF16), w, preferred_element_type=F32).astype(BF16)


def _fourier(uf, csl, csc, c64, s64, wbd, lc):
    b, s, w = uf.shape
    l = s - lc
    tmf = lc
    nt = 1 + l // tmf
    const = lambda bb, i: (0, 0)
    return pl.pallas_call(
        functools.partial(_fnet_kernel, lc=lc, l=l),
        out_shape=jax.ShapeDtypeStruct((b, s, w), BF16),
        grid=(b, nt),
        in_specs=[pl.BlockSpec((1, s, w), lambda bb, i: (bb, 0, 0)),
                  pl.BlockSpec((tmf, 2 * l), lambda bb, i: (jnp.maximum(i - 1, 0), 0)),
                  pl.BlockSpec((lc, 2 * lc), const),
                  pl.BlockSpec((w, w), const),
                  pl.BlockSpec((w, w), const),
                  pl.BlockSpec((w, w), const)],
        out_specs=pl.BlockSpec((1, tmf, w), lambda bb, i: (bb, i, 0)),
        scratch_shapes=[pltpu.VMEM((2 * l, w), BF16), pltpu.VMEM((2 * lc, w), BF16)],
        compiler_params=_params("arbitrary", "arbitrary"),
    )(uf, csl, csc, c64, s64, wbd)


def _gelu_tanh(x):
    return 0.5 * x * (1.0 + jnp.tanh(math.sqrt(2.0 / math.pi) * (x + 0.044715 * (x * x * x))))


def _route(logits_t, b_router_ref):
    e_all = range(N_EXPERTS)
    aff = [jax.nn.sigmoid(logits_t[e:e + 1, :]) for e in e_all]
    sel = [aff[e] + b_router_ref[e] for e in e_all]
    epg = EXPERTS_PER_GROUP
    keep = []
    for e in e_all:
        g0 = (e // epg) * epg
        rank = jnp.zeros_like(sel[e], dtype=jnp.int32)
        for o in range(g0, g0 + epg):
            if o == e:
                continue
            ahead = (sel[o] >= sel[e]) if o < e else (sel[o] > sel[e])
            rank = rank + ahead.astype(jnp.int32)
        keep.append(rank < 2)
    score = []
    for g in range(N_EXPERT_GROUPS):
        s0, s1, s2, s3 = sel[g * epg:(g + 1) * epg]
        hi01, lo01 = jnp.maximum(s0, s1), jnp.minimum(s0, s1)
        hi23, lo23 = jnp.maximum(s2, s3), jnp.minimum(s2, s3)
        top1 = jnp.maximum(hi01, hi23)
        top2 = jnp.maximum(jnp.minimum(hi01, hi23), jnp.maximum(lo01, lo23))
        score.append(top1 + top2)
    chosen = []
    for g in range(N_EXPERT_GROUPS):
        best = None
        for o in range(N_EXPERT_GROUPS):
            if o == g:
                continue
            cond = (score[g] > score[o]) if o < g else (score[g] >= score[o])
            best = cond if best is None else (best & cond)
        chosen.append(best)
    picked = [jnp.where(chosen[e // epg] & keep[e], aff[e], 0.0) for e in e_all]
    total = picked[0]
    for e in range(1, N_EXPERTS):
        total = total + picked[e]
    return [p / total for p in picked]


def _outproj_kernel(x_ref, att_ref, y_ref, us_ref, fou_ref, mod_ref, sd_ref, wglu_ref, bglu_ref,
                    wout_ref, g2_ref, wr_ref, br_ref, ident_ref,
                    x1_ref, h2_ref, gate_ref):
    mod = mod_ref[0, 0]
    y = sd_ref[...] * us_ref[...] + y_ref[0] + y_ref[1]
    g = _gelu_tanh(y)
    ssm = g * jax.nn.sigmoid(jnp.dot(g.astype(BF16), wglu_ref[...], preferred_element_type=F32)
                             + bglu_ref[...])
    a0, a1 = NA_WIDTH, NA_WIDTH + S5_WIDTH
    o = (jnp.dot(att_ref[0], wout_ref[0:a0, :], preferred_element_type=F32)
         + jnp.dot(ssm.astype(BF16), wout_ref[a0:a1, :], preferred_element_type=F32)
         + jnp.dot(fou_ref[0], wout_ref[a1:, :], preferred_element_type=F32))
    x1 = x_ref[0] + mod[2:3] * o
    x1_ref[0] = x1
    r = lax.rsqrt(jnp.mean(x1 * x1, axis=-1, keepdims=True) + EPS)
    h2 = (x1 * r * g2_ref[...]) * (1.0 + mod[4:5]) + mod[3:4]
    h2_ref[0] = h2.astype(BF16)

    logits_t = _dot_f32(wr_ref[...], h2, NT_DIMS)
    gates = _route(logits_t, br_ref)
    tm = h2.shape[0]
    sub = lax.broadcasted_iota(jnp.int32, (LANES, tm), 0)
    gt = jnp.zeros((LANES, tm), F32)
    for e in range(N_EXPERTS):
        gt = jnp.where(sub == e, gates[e], gt)
    p0 = gt.astype(BF16)
    r1 = gt - p0.astype(F32)
    p1 = r1.astype(BF16)
    p2 = (r1 - p1.astype(F32)).astype(BF16)
    ident = ident_ref[...]
    nt = functools.partial(lax.dot_general, dimension_numbers=NT_DIMS, preferred_element_type=F32)
    gate_ref[0] = nt(ident, p0) + nt(ident, p1) + nt(ident, p2)


def _out_projection(xa, att, y2, us, fou, modtab, s5_d, w_glu_bf, b_glu, w_out_bf, g2, w_router_t, b_router,
                    ident, tm, nct):
    b, s, d = xa.shape
    nt = s // tm
    row = lambda i, bb: (bb, i, 0)
    const2 = lambda i, bb: (0, 0)
    return pl.pallas_call(
        _outproj_kernel,
        out_shape=(jax.ShapeDtypeStruct((b, s, d), F32),
                   jax.ShapeDtypeStruct((b, s, d), BF16),
                   jax.ShapeDtypeStruct((b, s, LANES), F32)),
        grid=(nt, b),
        in_specs=[pl.BlockSpec((1, tm, d), row),
                  pl.BlockSpec((1, tm, NA_WIDTH), row),
                  pl.BlockSpec((2, tm, S5_WIDTH), lambda i, bb: (0, i, bb)),
                  pl.BlockSpec((tm, S5_WIDTH), lambda i, bb: (i, bb)),
                  pl.BlockSpec((1, tm, FNET_WIDTH), row),
                  pl.BlockSpec((1, 1, 6, d), lambda i, bb: (bb, jnp.where(i >= nct, 1, 0), 0, 0)),
                  pl.BlockSpec((1, S5_WIDTH), const2),
                  pl.BlockSpec((S5_WIDTH, S5_WIDTH), const2),
                  pl.BlockSpec((1, S5_WIDTH), const2),
                  pl.BlockSpec((d, d), const2),
                  pl.BlockSpec((1, d), const2),
                  pl.BlockSpec((N_EXPERTS, d), const2),
                  pl.BlockSpec(memory_space=pltpu.SMEM),
                  pl.BlockSpec((tm, tm), const2)],
        out_specs=(pl.BlockSpec((1, tm, d), row),
                   pl.BlockSpec((1, tm, d), row),
                   pl.BlockSpec((1, tm, LANES), row)),
        compiler_params=_params("arbitrary", "arbitrary"),
    )(xa, att, y2, us, fou, modtab, s5_d, w_glu_bf, b_glu, w_out_bf, g2, w_router_t, b_router, ident)


def _moe_kernel(h_ref, gate_ref, x1_ref, mod_ref, wg_ref, wu_ref, wd_ref, o_ref, acc_ref, *, lc, tm):
    s = pl.program_id(1)
    e = pl.program_id(2)

    @pl.when(e == 0)
    def _():
        acc_ref[...] = jnp.zeros_like(acc_ref)

    h = h_ref[0]
    lane = lax.broadcasted_iota(jnp.int32, (1, LANES), 1)
    ge = jnp.sum(jnp.where(lane == e, gate_ref[0], 0.0), axis=-1, keepdims=True)
    gp = jnp.dot(h, wg_ref[0], preferred_element_type=F32)
    up = jnp.dot(h, wu_ref[0], preferred_element_type=F32)
    he = (gp * jax.nn.sigmoid(gp)) * up * ge
    acc_ref[...] += jnp.dot(he.astype(BF16), wd_ref[0], preferred_element_type=F32)

    @pl.when(e == N_EXPERTS - 1)
    def _():
        rowi = s * tm + lax.broadcasted_iota(jnp.int32, (tm, 1), 0)
        m5 = jnp.where(rowi < lc, mod_ref[0, 0, 5:6, :], mod_ref[0, 1, 5:6, :])
        o_ref[0] = x1_ref[0] + m5 * acc_ref[...]


def _experts(h2, gates, x1, modtab, wg_bf, wu_bf, wd_bf, lc, tm):
    b, s, d = x1.shape
    ns = s // tm
    row = lambda bb, i, e: (bb, i, 0)
    per_e = lambda bb, i, e: (e, 0, 0)
    return pl.pallas_call(
        functools.partial(_moe_kernel, lc=lc, tm=tm),
        out_shape=jax.ShapeDtypeStruct((b, s, d), F32),
        grid=(b, ns, N_EXPERTS),
        in_specs=[pl.BlockSpec((1, tm, d), row),
                  pl.BlockSpec((1, tm, LANES), row),
                  pl.BlockSpec((1, tm, d), row),
                  pl.BlockSpec((1, 2, 6, d), lambda bb, i, e: (bb, 0, 0, 0)),
                  pl.BlockSpec((1, d, EXPERT_FF), per_e),
                  pl.BlockSpec((1, d, EXPERT_FF), per_e),
                  pl.BlockSpec((1, EXPERT_FF, d), per_e)],
        out_specs=pl.BlockSpec((1, tm, d), row),
        scratch_shapes=[pltpu.VMEM((tm, d), F32)],
        compiler_params=_params("arbitrary", "arbitrary", "arbitrary"),
    )(h2, gates, x1, modtab, wg_bf, wu_bf, wd_bf)


def _rope_tables(l, lc):
    t = jnp.arange(l, dtype=jnp.int32)
    row = (t // GRID_W).astype(F32)
    col = (t % GRID_W).astype(F32)
    inv_freq = ROPE_BASE ** (-jnp.arange(ROPE_PAIRS_PER_AXIS, dtype=F32) / ROPE_PAIRS_PER_AXIS)
    ang = jnp.concatenate([row[:, None] * inv_freq, col[:, None] * inv_freq], axis=-1)
    cos, sin = jnp.cos(ang), jnp.sin(ang)
    zero = jnp.zeros_like(sin)
    cos_h = jnp.concatenate([cos, cos], axis=-1)
    sin_lo = jnp.concatenate([-sin, zero], axis=-1)
    sin_hi = jnp.concatenate([zero, sin], axis=-1)
    def full(tab, ctx_value):
        tab = jnp.tile(tab, (1, NA_HEADS))
        return jnp.concatenate([jnp.full((lc, NA_WIDTH), ctx_value, F32), tab], axis=0)
    return full(cos_h, 1.0), full(sin_lo, 0.0), full(sin_hi, 0.0)


def _largest_tile(n, limit, multiple):
    best = multiple
    for t in range(multiple, limit + 1, multiple):
        if n % t == 0:
            best = t
    return best


def kernel(x, c, ctx, c_ctx, w_ada, b_ada, norm1_g, norm2_g, w_in, q_norm_g, k_norm_g, rpb, s5_lam_re, s5_lam_im, s5_log_step, s5_b_re, s5_b_im, s5_c_re, s5_c_im, s5_d, w_glu, b_glu, w_fnet, w_out, w_router, b_router, w_gate, w_up, w_down):
    b, l, d = x.shape
    lc = ctx.shape[1]
    s = lc + l
    depth = w_ada.shape[0]
    rows = l // GRID_W
    tm = min(256, lc)
    assert lc % tm == 0 and l % tm == 0 and l % GRID_W == 0
    nct = lc // tm
    rb = 4 if rows % 4 == 0 else 1
    tc = min(32, lc)
    tm_moe = _largest_tile(s, 768, 2 * SUBLANES)

    xa = jnp.concatenate([ctx, x], axis=1)

    n_mod = -(-(b + 1) // SUBLANES) * SUBLANES
    cvec = jnp.concatenate([c, c_ctx[None, :], jnp.zeros((n_mod - b - 1, d), F32)], axis=0)
    mod_all = _modulation(cvec, w_ada, b_ada)

    cos, sin_lo, sin_hi = _rope_tables(l, lc)
    head_ones = jnp.asarray(np.kron(np.eye(NA_HEADS), np.ones((HEAD_DIM, HEAD_DIM))), dtype=BF16)
    csl, csc, c64, s64 = _fnet_constants(l, lc)
    ident = jnp.eye(tm, dtype=BF16)
    w_router_t = w_router.T

    for layer in range(depth):
        mod = mod_all[layer]
        modtab = jnp.stack([jnp.broadcast_to(mod[b].reshape(1, 6, d), (b, 6, d)),
                            mod[:b].reshape(b, 6, d)], axis=1)
        qg = jnp.tile(q_norm_g[layer] * ATTN_SCALE, NA_HEADS).reshape(1, NA_WIDTH)
        kg = jnp.tile(k_norm_g[layer], NA_HEADS).reshape(1, NA_WIDTH)
        q, k, v, us, uf = _in_projection(xa, modtab, norm1_g[layer].reshape(1, d), w_in[layer].astype(BF16),
                                         qg, kg, head_ones, cos, sin_lo, sin_hi, tm, nct)

        bias = _attn_bias_tables(rpb[layer], rows, rb)
        att = _attention(q, k, v, bias, lc, rows, rb)

        bmat, cmat, a_re, a_im = _s5_matrices(s5_lam_re[layer], s5_lam_im[layer], s5_log_step[layer],
                                              s5_b_re[layer], s5_b_im[layer], s5_c_re[layer], s5_c_im[layer], b)
        y = _s5_scan(us.reshape(s * b, S5_WIDTH), bmat, cmat, a_re, a_im, b, lc, tc)

        wbd = jnp.einsum('hde,hg->hdge', w_fnet[layer], jnp.eye(FNET_HEADS, dtype=F32))
        fou = _fourier(uf, csl, csc, c64, s64, wbd.reshape(FNET_WIDTH, FNET_WIDTH).astype(BF16), lc)

        x1, h2, gates = _out_projection(
            xa, att, y.reshape(2, s, b * S5_WIDTH), us, fou, modtab, s5_d[layer].reshape(1, S5_WIDTH),
            w_glu[layer].astype(BF16), b_glu[layer].reshape(1, S5_WIDTH), w_out[layer].astype(BF16),
            norm2_g[layer].reshape(1, d), w_router_t, b_router, ident, tm, nct)

        xa = _experts(h2, gates, x1, modtab, w_gate[layer].astype(BF16), w_up[layer].astype(BF16),
                      w_down[layer].astype(BF16), lc, tm_moe)

    return xa[:, lc:]
```

```python
import functools
import math

import numpy as np
import jax
import jax.numpy as jnp
from jax import lax
from jax.experimental import pallas as pl
from jax.experimental.pallas import tpu as pltpu

F32 = jnp.float32
BF16 = jnp.bfloat16

GRID_W = 64
EPS = 1e-6
HEAD_DIM = 64
NA_WIDTH = 512
NA_HEADS = 8
NA_WIN_H = 8
NA_WIN_W = 16
S5_WIDTH = 256
S5_GROUP = 16
S5_GROUPS = 16
S5_STATE = 64
S5_COLS = S5_GROUPS * S5_STATE
FNET_WIDTH = 256
FNET_HEADS = 4
FNET_HEAD_DIM = 64
IN_COLS = 3 * NA_WIDTH + S5_WIDTH + FNET_WIDTH
ATTN_SCALE = HEAD_DIM ** -0.5
ROPE_BASE = 100.0
ROPE_PAIRS_PER_AXIS = HEAD_DIM // 4
N_EXPERTS = 16
N_EXPERT_GROUPS = 4
EXPERTS_PER_GROUP = 4
EXPERT_FF = 512

LANES = 128
SUBLANES = 8
VMEM_LIMIT = 56 * 1024 * 1024

NEG_BIG = -1e30
NT_DIMS = (((1,), (1,)), ((), ()))


def _split_bf16(x):
    hi = x.astype(BF16)
    lo = (x - hi.astype(F32)).astype(BF16)
    return hi, lo


def _dot_f32(a, b, dims=None):
    a_hi, a_lo = _split_bf16(a)
    b_hi, b_lo = _split_bf16(b)
    if dims is None:
        dot = functools.partial(jnp.dot, preferred_element_type=F32)
    else:
        dot = functools.partial(lax.dot_general, dimension_numbers=dims, preferred_element_type=F32)
    return dot(a_hi, b_hi) + dot(a_hi, b_lo) + dot(a_lo, b_hi)


def _params(*sem):
    return pltpu.CompilerParams(dimension_semantics=sem, vmem_limit_bytes=VMEM_LIMIT)


def _mod_kernel(c_ref, w_ref, b_ref, o_ref):
    c = c_ref[...]
    s = c * jax.nn.sigmoid(c)
    o_ref[0] = _dot_f32(s, w_ref[0]) + b_ref[0]


def _modulation(cvec, w_ada, b_ada):
    depth, d, n = w_ada.shape
    r = cvec.shape[0]
    tn = 768
    return pl.pallas_call(
        _mod_kernel,
        out_shape=jax.ShapeDtypeStruct((depth, r, n), F32),
        grid=(depth, n // tn),
        in_specs=[pl.BlockSpec((r, d), lambda l, j: (0, 0)),
                  pl.BlockSpec((1, d, tn), lambda l, j: (l, 0, j)),
                  pl.BlockSpec((1, 1, tn), lambda l, j: (l, 0, j))],
        out_specs=pl.BlockSpec((1, r, tn), lambda l, j: (l, 0, j)),
        compiler_params=_params("arbitrary", "arbitrary"),
    )(cvec, w_ada, b_ada.reshape(depth, 1, n))


def _inproj_kernel(x_ref, mod_ref, g1_ref, w_ref, qg_ref, kg_ref, gm_ref, cos_ref, sa_ref, sb_ref,
                   q_ref, k_ref, v_ref, us_ref, uf_ref):
    x = x_ref[0]
    mod = mod_ref[0, 0]
    r = lax.rsqrt(jnp.mean(x * x, axis=-1, keepdims=True) + EPS)
    h = (x * r * g1_ref[...]) * (1.0 + mod[1:2]) + mod[0:1]
    proj = jnp.dot(h.astype(BF16), w_ref[...], preferred_element_type=F32)

    cos = cos_ref[...]
    sin_lo = sa_ref[...]
    sin_hi = sb_ref[...]
    gm = gm_ref[...]
    nw = NA_WIDTH

    def head_norm_rope(t, g):
        ss = jnp.dot((t * t).astype(BF16), gm, preferred_element_type=F32)
        tn = t * lax.rsqrt(ss * (1.0 / HEAD_DIM) + EPS) * g
        half = HEAD_DIM // 2
        return (tn * cos + pltpu.roll(tn, nw - half, 1) * sin_lo + pltpu.roll(tn, half, 1) * sin_hi)

    q_ref[0] = head_norm_rope(proj[:, 0:nw], qg_ref[...]).astype(BF16)
    k_ref[0] = head_norm_rope(proj[:, nw:2 * nw], kg_ref[...]).astype(BF16)
    v_ref[0] = proj[:, 2 * nw:3 * nw].astype(BF16)
    us_ref[...] = proj[:, 3 * nw:3 * nw + S5_WIDTH]
    uf_ref[0] = proj[:, 3 * nw + S5_WIDTH:].astype(BF16)


def _in_projection(xa, modtab, g1, w_in_bf, qg, kg, gm, cos, sin_lo, sin_hi, tm, nct):
    b, s, d = xa.shape
    nt = s // tm
    nw = NA_WIDTH
    row = lambda i, bb: (bb, i, 0)
    const2 = lambda i, bb: (0, 0)
    tab = lambda i, bb: (i, 0)
    return pl.pallas_call(
        _inproj_kernel,
        out_shape=(jax.ShapeDtypeStruct((b, s, nw), BF16),
                   jax.ShapeDtypeStruct((b, s, nw), BF16),
                   jax.ShapeDtypeStruct((b, s, nw), BF16),
                   jax.ShapeDtypeStruct((s, b * S5_WIDTH), F32),
                   jax.ShapeDtypeStruct((b, s, FNET_WIDTH), BF16)),
        grid=(nt, b),
        in_specs=[pl.BlockSpec((1, tm, d), row),
                  pl.BlockSpec((1, 1, 6, d), lambda i, bb: (bb, jnp.where(i >= nct, 1, 0), 0, 0)),
                  pl.BlockSpec((1, d), const2),
                  pl.BlockSpec((d, IN_COLS), const2),
                  pl.BlockSpec((1, nw), const2),
                  pl.BlockSpec((1, nw), const2),
                  pl.BlockSpec((nw, nw), const2),
                  pl.BlockSpec((tm, nw), tab),
                  pl.BlockSpec((tm, nw), tab),
                  pl.BlockSpec((tm, nw), tab)],
        out_specs=(pl.BlockSpec((1, tm, nw), row),
                   pl.BlockSpec((1, tm, nw), row),
                   pl.BlockSpec((1, tm, nw), row),
                   pl.BlockSpec((tm, S5_WIDTH), lambda i, bb: (i, bb)),
                   pl.BlockSpec((1, tm, FNET_WIDTH), row)),
        compiler_params=_params("arbitrary", "arbitrary"),
    )(xa, modtab, g1, w_in_bf, qg, kg, gm, cos, sin_lo, sin_hi)


def _attn_blocks(rows, rb):
    kh = min(NA_WIN_H, rows)
    kr = min(rb + kh - 1, rows)
    blocks, classes = [], {}
    for r0 in range(0, rows, rb):
        ks = min(max(r0 - kh // 2, 0), rows - kr)
        starts = tuple(min(max(r - kh // 2, 0), rows - kh) - ks for r in range(r0, r0 + rb))
        key = (r0 - ks, starts)
        cls = classes.setdefault(key, len(classes))
        blocks.append((r0, ks, cls))
    return kh, kr, blocks, list(classes.keys())


def _attn_bias_tables(rpb, rows, rb):
    kh, kr, blocks, class_keys = _attn_blocks(rows, rb)
    w = GRID_W
    col = np.arange(w)
    col_start = np.clip(col - NA_WIN_W // 2, 0, w - NA_WIN_W)
    n_row, n_col = 2 * NA_WIN_H - 1, 2 * NA_WIN_W - 1
    col_off = np.clip(col[None, :] - col[:, None] + (NA_WIN_W - 1), 0, n_col - 1)
    col_pick = jnp.asarray(np.eye(n_col)[col_off].reshape(w * w, n_col), dtype=F32)
    col_valid = (col[None, :] >= col_start[:, None]) & (col[None, :] < col_start[:, None] + NA_WIN_W)
    tabs = []
    for q_minus_ks, starts in class_keys:
        rq = np.arange(rb)[:, None]
        i = np.arange(kr)[None, :]
        rs = np.asarray(starts)[:, None]
        row_valid = (i >= rs) & (i < rs + kh)
        row_off = np.clip(i - (q_minus_ks + rq) + (NA_WIN_H - 1), 0, n_row - 1)
        row_pick = jnp.asarray(np.eye(n_row)[row_off].reshape(rb * kr, n_row), dtype=F32)
        bias = jnp.einsum('hyx,ny,mx->hnm', rpb.astype(F32), row_pick, col_pick,
                          precision=lax.Precision.HIGHEST)
        bias = bias.reshape(-1, rb, kr, w, w).transpose(0, 1, 3, 2, 4)
        valid = row_valid[:, None, :, None] & col_valid[None, :, None, :]
        tabs.append(jnp.where(valid[None], bias, NEG_BIG).reshape(-1, rb * w, kr * w))
    return jnp.stack(tabs, axis=0)


def _attn_kernel(q_ref, k_ref, v_ref, bias_ref, o_ref, *, lc, rb, kr, blocks):
    lane = lax.broadcasted_iota(jnp.int32, (1, LANES), 1)
    first_head = lane < HEAD_DIM

    def stack_heads(qb):
        z = jnp.zeros_like(qb)
        return jnp.concatenate([jnp.where(first_head, qb, z), jnp.where(first_head, z, qb)], axis=0)

    def unstack_heads(o, n):
        return jnp.where(first_head, o[:n], o[n:])

    kc = k_ref[0, 0:lc, :]
    vc = v_ref[0, 0:lc, :]

    s = lax.dot_general(stack_heads(q_ref[0, 0:lc, :]), kc, NT_DIMS, preferred_element_type=F32)
    m = jnp.max(s, axis=-1, keepdims=True)
    p = jnp.exp(s - m)
    den = jnp.sum(p, axis=-1, keepdims=True)
    o = jnp.dot(p.astype(BF16), vc, preferred_element_type=F32) / den
    o_ref[0, 0:lc, :] = unstack_heads(o, lc).astype(BF16)

    nq = rb * GRID_W
    nk = kr * GRID_W
    for r0, ks, cls in blocks:
        q0 = lc + r0 * GRID_W
        k0 = lc + ks * GRID_W
        q2 = stack_heads(q_ref[0, q0:q0 + nq, :])
        k_loc = k_ref[0, k0:k0 + nk, :]
        v_loc = v_ref[0, k0:k0 + nk, :]
        s_loc = lax.dot_general(q2, k_loc, NT_DIMS, preferred_element_type=F32)
        s_loc = s_loc + bias_ref[cls].reshape(2 * nq, nk)
        s_ctx = lax.dot_general(q2, kc, NT_DIMS, preferred_element_type=F32)
        m = jnp.maximum(jnp.max(s_loc, axis=-1, keepdims=True), jnp.max(s_ctx, axis=-1, keepdims=True))
        p_loc = jnp.exp(s_loc - m)
        p_ctx = jnp.exp(s_ctx - m)
        den = jnp.sum(p_loc, axis=-1, keepdims=True) + jnp.sum(p_ctx, axis=-1, keepdims=True)
        o = (jnp.dot(p_loc.astype(BF16), v_loc, preferred_element_type=F32)
             + jnp.dot(p_ctx.astype(BF16), vc, preferred_element_type=F32)) / den
        o_ref[0, q0:q0 + nq, :] = unstack_heads(o, nq).astype(BF16)


def _attention(q, k, v, bias, lc, rows, rb):
    b, s, nw = q.shape
    _, kr, blocks, _ = _attn_blocks(rows, rb)
    ncls = bias.shape[0]
    spec = pl.BlockSpec((1, s, LANES), lambda bb, p: (bb, 0, p))
    return pl.pallas_call(
        functools.partial(_attn_kernel, lc=lc, rb=rb, kr=kr, blocks=blocks),
        out_shape=jax.ShapeDtypeStruct((b, s, nw), BF16),
        grid=(b, nw // LANES),
        in_specs=[spec, spec, spec,
                  pl.BlockSpec((ncls, 2, rb * GRID_W, kr * GRID_W), lambda bb, p: (0, p, 0, 0))],
        out_specs=spec,
        compiler_params=_params("arbitrary", "arbitrary"),
    )(q, k, v, bias)


def _s5_matrices(lam_re, lam_im, log_step, b_re, b_im, c_re, c_im, batch):
    lam_re = lam_re.astype(F32)
    lam_im = lam_im.astype(F32)
    step = jnp.exp(log_step.astype(F32))[..., None]
    mag = jnp.exp(lam_re * step)
    a_re = mag * jnp.cos(lam_im * step)
    a_im = mag * jnp.sin(lam_im * step)
    den = lam_re * lam_re + lam_im * lam_im
    co_re = ((a_re - 1.0) * lam_re + a_im * lam_im) / den
    co_im = (a_im * lam_re - (a_re - 1.0) * lam_im) / den
    b_re = b_re.astype(F32)
    b_im = b_im.astype(F32)
    bb_re = co_re[..., None] * b_re - co_im[..., None] * b_im
    bb_im = co_re[..., None] * b_im + co_im[..., None] * b_re
    eye = jnp.eye(S5_GROUPS, dtype=F32)
    to_in = lambda t: jnp.einsum('dgpc,gh->dgchp', t, eye).reshape(2, S5_WIDTH, S5_COLS)
    bmat = jnp.concatenate([to_in(bb_re), to_in(bb_im)], axis=-1)
    to_out = lambda t: jnp.einsum('dgcp,gh->dgphc', t.astype(F32), eye).reshape(2, S5_COLS, S5_WIDTH)
    cmat = jnp.concatenate([to_out(c_re), -to_out(c_im)], axis=1)
    bc = lambda t: jnp.broadcast_to(t.reshape(2, 1, S5_COLS), (2, batch, S5_COLS))
    return bmat.astype(BF16), cmat.astype(BF16), bc(a_re), bc(a_im)


def _s5_kernel(u_ref, bm_ref, cm_ref, ar_ref, ai_ref, y_ref, hbuf, st_re, st_im, *, tc, batch):
    d = pl.program_id(0)
    j = pl.program_id(1)

    @pl.when(j == 0)
    def _():
        st_re[...] = jnp.zeros_like(st_re)
        st_im[...] = jnp.zeros_like(st_im)

    hbuf[...] = jnp.dot(u_ref[...].astype(BF16), bm_ref[0], preferred_element_type=F32)

    ncol = 256
    for cg in range(S5_COLS // ncol):
        re = slice(cg * ncol, (cg + 1) * ncol)
        im = slice(S5_COLS + cg * ncol, S5_COLS + (cg + 1) * ncol)
        a_re = ar_ref[0, :, re]
        a_im = ai_ref[0, :, re]

        def step(tt, carry, re=re, im=im, a_re=a_re, a_im=a_im):
            h_re, h_im = carry
            t = jnp.where(d == 0, tt, tc - 1 - tt)
            row = pl.multiple_of(t * batch, batch)
            n_re = a_re * h_re - a_im * h_im + hbuf[pl.ds(row, batch), re]
            n_im = a_re * h_im + a_im * h_re + hbuf[pl.ds(row, batch), im]
            hbuf[pl.ds(row, batch), re] = n_re
            hbuf[pl.ds(row, batch), im] = n_im
            return n_re, n_im

        h_re, h_im = lax.fori_loop(0, tc, step, (st_re[:, re], st_im[:, re]), unroll=4)
        st_re[:, re] = h_re
        st_im[:, re] = h_im

    y_ref[0] = jnp.dot(hbuf[...].astype(BF16), cm_ref[0], preferred_element_type=F32)


def _s5_scan(us2, bmat, cmat, a_re, a_im, batch, lc, tc):
    n, w = us2.shape
    s = n // batch
    nch = s // tc
    nch_c = lc // tc

    def pos(d, j):
        back = jnp.where(j < nch_c, nch_c - 1 - j, nch + nch_c - 1 - j)
        return jnp.where(d == 0, j, back)

    per_dir = lambda d, j: (d, 0, 0)
    return pl.pallas_call(
        functools.partial(_s5_kernel, tc=tc, batch=batch),
        out_shape=jax.ShapeDtypeStruct((2, n, w), F32),
        grid=(2, nch),
        in_specs=[pl.BlockSpec((tc * batch, w), lambda d, j: (pos(d, j), 0)),
                  pl.BlockSpec((1, w, 2 * S5_COLS), per_dir),
                  pl.BlockSpec((1, 2 * S5_COLS, w), per_dir),
                  pl.BlockSpec((1, batch, S5_COLS), per_dir),
                  pl.BlockSpec((1, batch, S5_COLS), per_dir)],
        out_specs=pl.BlockSpec((1, tc * batch, w), lambda d, j: (d, pos(d, j), 0)),
        scratch_shapes=[pltpu.VMEM((tc * batch, 2 * S5_COLS), F32),
                        pltpu.VMEM((batch, S5_COLS), F32),
                        pltpu.VMEM((batch, S5_COLS), F32)],
        compiler_params=_params("arbitrary", "arbitrary"),
    )(us2, bmat, cmat, a_re, a_im)


def _dft_cos_sin(n):
    k = np.arange(n, dtype=np.int64)
    ang = 2.0 * np.pi * ((k[:, None] * k[None, :]) % n).astype(np.float64) / n
    return np.cos(ang), np.sin(ang)


def _fnet_constants(l, lc):
    def seq_matrix(n):
        c, s = _dft_cos_sin(n)
        scale = 1.0 / math.sqrt(n * FNET_HEAD_DIM)
        return jnp.asarray(np.concatenate([c, -s], axis=1) * scale, dtype=F32).astype(BF16)
    c64, s64 = _dft_cos_sin(FNET_HEAD_DIM)
    eye = np.eye(FNET_HEADS)
    return (seq_matrix(l), seq_matrix(lc),
            jnp.asarray(np.kron(eye, c64), dtype=F32).astype(BF16),
            jnp.asarray(np.kron(eye, s64), dtype=F32).astype(BF16))


def _fnet_kernel(u_ref, csl_ref, csc_ref, c64_ref, s64_ref, w_ref, o_ref, tl_ref, tc_ref, *, lc, l):
    i = pl.program_id(1)
    w = w_ref[...]

    @pl.when(i == 0)
    def _():
        u = u_ref[0]
        t_cos = jnp.dot(u, c64_ref[...], preferred_element_type=F32).astype(BF16)
        t_sin = jnp.dot(u, s64_ref[...], preferred_element_type=F32).astype(BF16)
        tc_ref[0:lc, :] = t_cos[0:lc]
        tc_ref[lc:2 * lc, :] = t_sin[0:lc]
        tl_ref[0:l, :] = t_cos[lc:]
        tl_ref[l:2 * l, :] = t_sin[lc:]
        z = jnp.dot(csc_ref[...], tc_ref[...], preferred_element_type=F32)
        o_ref[0] = jnp.dot(z.astype(BF16), w, preferred_element_type=F32).astype(BF16)

    @pl.when(i > 0)
    def _():
        z = jnp.dot(csl_ref[...], tl_ref[...], preferred_element_type=F32)
        o_ref[0] = jnp.dot(z.astype(BF16), w, preferred_element_type=F32).astype(BF16)


def _fourier(uf, csl, csc, c64, s64, wbd, lc):
    b, s, w = uf.shape
    l = s - lc
    tmf = lc
    nt = 1 + l // tmf
    const = lambda bb, i: (0, 0)
    return pl.pallas_call(
        functools.partial(_fnet_kernel, lc=lc, l=l),
        out_shape=jax.ShapeDtypeStruct((b, s, w), BF16),
        grid=(b, nt),
        in_specs=[pl.BlockSpec((1, s, w), lambda bb, i: (bb, 0, 0)),
                  pl.BlockSpec((tmf, 2 * l), lambda bb, i: (jnp.maximum(i - 1, 0), 0)),
                  pl.BlockSpec((lc, 2 * lc), const),
                  pl.BlockSpec((w, w), const),
                  pl.BlockSpec((w, w), const),
                  pl.BlockSpec((w, w), const)],
        out_specs=pl.BlockSpec((1, tmf, w), lambda bb, i: (bb, i, 0)),
        scratch_shapes=[pltpu.VMEM((2 * l, w), BF16), pltpu.VMEM((2 * lc, w), BF16)],
        compiler_params=_params("arbitrary", "arbitrary"),
    )(uf, csl, csc, c64, s64, wbd)


def _gelu_tanh(x):
    return 0.5 * x * (1.0 + jnp.tanh(math.sqrt(2.0 / math.pi) * (x + 0.044715 * (x * x * x))))


def _route(logits_t, b_router_ref):
    e_all = range(N_EXPERTS)
    aff = [jax.nn.sigmoid(logits_t[e:e + 1, :]) for e in e_all]
    sel = [aff[e] + b_router_ref[e] for e in e_all]
    epg = EXPERTS_PER_GROUP
    keep = []
    for e in e_all:
        g0 = (e // epg) * epg
        rank = jnp.zeros_like(sel[e], dtype=jnp.int32)
        for o in range(g0, g0 + epg):
            if o == e:
                continue
            ahead = (sel[o] >= sel[e]) if o < e else (sel[o] > sel[e])
            rank = rank + ahead.astype(jnp.int32)
        keep.append(rank < 2)
    score = []
    for g in range(N_EXPERT_GROUPS):
        s0, s1, s2, s3 = sel[g * epg:(g + 1) * epg]
        hi01, lo01 = jnp.maximum(s0, s1), jnp.minimum(s0, s1)
        hi23, lo23 = jnp.maximum(s2, s3), jnp.minimum(s2, s3)
        top1 = jnp.maximum(hi01, hi23)
        top2 = jnp.maximum(jnp.minimum(hi01, hi23), jnp.maximum(lo01, lo23))
        score.append(top1 + top2)
    chosen = []
    for g in range(N_EXPERT_GROUPS):
        best = None
        for o in range(N_EXPERT_GROUPS):
            if o == g:
                continue
            cond = (score[g] > score[o]) if o < g else (score[g] >= score[o])
            best = cond if best is None else (best & cond)
        chosen.append(best)
    picked = [jnp.where(chosen[e // epg] & keep[e], aff[e], 0.0) for e in e_all]
    total = picked[0]
    for e in range(1, N_EXPERTS):
        total = total + picked[e]
    return [p / total for p in picked]


def _outproj_kernel(x_ref, att_ref, y_ref, us_ref, fou_ref, mod_ref, sd_ref, wglu_ref, bglu_ref,
                    wout_ref, g2_ref, wr_ref, br_ref, ident_ref,
                    x1_ref, h2_ref, gate_ref):
    mod = mod_ref[0, 0]
    y = sd_ref[...] * us_ref[...] + y_ref[0] + y_ref[1]
    g = _gelu_tanh(y)
    ssm = g * jax.nn.sigmoid(jnp.dot(g.astype(BF16), wglu_ref[...], preferred_element_type=F32)
                             + bglu_ref[...])
    a0, a1 = NA_WIDTH, NA_WIDTH + S5_WIDTH
    o = (jnp.dot(att_ref[0], wout_ref[0:a0, :], preferred_element_type=F32)
         + jnp.dot(ssm.astype(BF16), wout_ref[a0:a1, :], preferred_element_type=F32)
         + jnp.dot(fou_ref[0], wout_ref[a1:, :], preferred_element_type=F32))
    x1 = x_ref[0] + mod[2:3] * o
    x1_ref[0] = x1
    r = lax.rsqrt(jnp.mean(x1 * x1, axis=-1, keepdims=True) + EPS)
    h2 = (x1 * r * g2_ref[...]) * (1.0 + mod[4:5]) + mod[3:4]
    h2_ref[0] = h2.astype(BF16)

    logits_t = _dot_f32(wr_ref[...], h2, NT_DIMS)
    gates = _route(logits_t, br_ref)
    tm = h2.shape[0]
    sub = lax.broadcasted_iota(jnp.int32, (LANES, tm), 0)
    gt = jnp.zeros((LANES, tm), F32)
    for e in range(N_EXPERTS):
        gt = jnp.where(sub == e, gates[e], gt)
    p0 = gt.astype(BF16)
    r1 = gt - p0.astype(F32)
    p1 = r1.astype(BF16)
    p2 = (r1 - p1.astype(F32)).astype(BF16)
    ident = ident_ref[...]
    nt = functools.partial(lax.dot_general, dimension_numbers=NT_DIMS, preferred_element_type=F32)
    gate_ref[0] = nt(ident, p0) + nt(ident, p1) + nt(ident, p2)


def _out_projection(xa, att, y2, us, fou, modtab, s5_d, w_glu_bf, b_glu, w_out_bf, g2, w_router_t, b_router,
                    ident, tm, nct):
    b, s, d = xa.shape
    nt = s // tm
    row = lambda i, bb: (bb, i, 0)
    const2 = lambda i, bb: (0, 0)
    return pl.pallas_call(
        _outproj_kernel,
        out_shape=(jax.ShapeDtypeStruct((b, s, d), F32),
                   jax.ShapeDtypeStruct((b, s, d), BF16),
                   jax.ShapeDtypeStruct((b, s, LANES), F32)),
        grid=(nt, b),
        in_specs=[pl.BlockSpec((1, tm, d), row),
                  pl.BlockSpec((1, tm, NA_WIDTH), row),
                  pl.BlockSpec((2, tm, S5_WIDTH), lambda i, bb: (0, i, bb)),
                  pl.BlockSpec((tm, S5_WIDTH), lambda i, bb: (i, bb)),
                  pl.BlockSpec((1, tm, FNET_WIDTH), row),
                  pl.BlockSpec((1, 1, 6, d), lambda i, bb: (bb, jnp.where(i >= nct, 1, 0), 0, 0)),
                  pl.BlockSpec((1, S5_WIDTH), const2),
                  pl.BlockSpec((S5_WIDTH, S5_WIDTH), const2),
                  pl.BlockSpec((1, S5_WIDTH), const2),
                  pl.BlockSpec((d, d), const2),
                  pl.BlockSpec((1, d), const2),
                  pl.BlockSpec((N_EXPERTS, d), const2),
                  pl.BlockSpec(memory_space=pltpu.SMEM),
                  pl.BlockSpec((tm, tm), const2)],
        out_specs=(pl.BlockSpec((1, tm, d), row),
                   pl.BlockSpec((1, tm, d), row),
                   pl.BlockSpec((1, tm, LANES), row)),
        compiler_params=_params("arbitrary", "arbitrary"),
    )(xa, att, y2, us, fou, modtab, s5_d, w_glu_bf, b_glu, w_out_bf, g2, w_router_t, b_router, ident)


def _moe_kernel(h_ref, gate_ref, x1_ref, mod_ref, wg_ref, wu_ref, wd_ref, o_ref, acc_ref, *, lc, tm):
    s = pl.program_id(1)
    e = pl.program_id(2)

    @pl.when(e == 0)
    def _():
        acc_ref[...] = jnp.zeros_like(acc_ref)

    h = h_ref[0]
    lane = lax.broadcasted_iota(jnp.int32, (1, LANES), 1)
    ge = jnp.sum(jnp.where(lane == e, gate_ref[0], 0.0), axis=-1, keepdims=True)
    gp = jnp.dot(h, wg_ref[0], preferred_element_type=F32)
    up = jnp.dot(h, wu_ref[0], preferred_element_type=F32)
    he = (gp * jax.nn.sigmoid(gp)) * up * ge
    acc_ref[...] += jnp.dot(he.astype(BF16), wd_ref[0], preferred_element_type=F32)

    @pl.when(e == N_EXPERTS - 1)
    def _():
        rowi = s * tm + lax.broadcasted_iota(jnp.int32, (tm, 1), 0)
        m5 = jnp.where(rowi < lc, mod_ref[0, 0, 5:6, :], mod_ref[0, 1, 5:6, :])
        o_ref[0] = x1_ref[0] + m5 * acc_ref[...]


def _experts(h2, gates, x1, modtab, wg_bf, wu_bf, wd_bf, lc, tm):
    b, s, d = x1.shape
    ns = s // tm
    row = lambda bb, i, e: (bb, i, 0)
    per_e = lambda bb, i, e: (e, 0, 0)
    return pl.pallas_call(
        functools.partial(_moe_kernel, lc=lc, tm=tm),
        out_shape=jax.ShapeDtypeStruct((b, s, d), F32),
        grid=(b, ns, N_EXPERTS),
        in_specs=[pl.BlockSpec((1, tm, d), row),
                  pl.BlockSpec((1, tm, LANES), row),
                  pl.BlockSpec((1, tm, d), row),
                  pl.BlockSpec((1, 2, 6, d), lambda bb, i, e: (bb, 0, 0, 0)),
                  pl.BlockSpec((1, d, EXPERT_FF), per_e),
                  pl.BlockSpec((1, d, EXPERT_FF), per_e),
                  pl.BlockSpec((1, EXPERT_FF, d), per_e)],
        out_specs=pl.BlockSpec((1, tm, d), row),
        scratch_shapes=[pltpu.VMEM((tm, d), F32)],
        compiler_params=_params("arbitrary", "arbitrary", "arbitrary"),
    )(h2, gates, x1, modtab, wg_bf, wu_bf, wd_bf)


def _rope_tables(l, lc):
    t = jnp.arange(l, dtype=jnp.int32)
    row = (t // GRID_W).astype(F32)
    col = (t % GRID_W).astype(F32)
    inv_freq = ROPE_BASE ** (-jnp.arange(ROPE_PAIRS_PER_AXIS, dtype=F32) / ROPE_PAIRS_PER_AXIS)
    ang = jnp.concatenate([row[:, None] * inv_freq, col[:, None] * inv_freq], axis=-1)
    cos, sin = jnp.cos(ang), jnp.sin(ang)
    zero = jnp.zeros_like(sin)
    cos_h = jnp.concatenate([cos, cos], axis=-1)
    sin_lo = jnp.concatenate([-sin, zero], axis=-1)
    sin_hi = jnp.concatenate([zero, sin], axis=-1)
    def full(tab, ctx_value):
        tab = jnp.tile(tab, (1, NA_HEADS))
        return jnp.concatenate([jnp.full((lc, NA_WIDTH), ctx_value, F32), tab], axis=0)
    return full(cos_h, 1.0), full(sin_lo, 0.0), full(sin_hi, 0.0)


def _largest_tile(n, limit, multiple):
    best = multiple
    for t in range(multiple, limit + 1, multiple):
        if n % t == 0:
            best = t
    return best


def kernel(x, c, ctx, c_ctx, w_ada, b_ada, norm1_g, norm2_g, w_in, q_norm_g, k_norm_g, rpb, s5_lam_re, s5_lam_im, s5_log_step, s5_b_re, s5_b_im, s5_c_re, s5_c_im, s5_d, w_glu, b_glu, w_fnet, w_out, w_router, b_router, w_gate, w_up, w_down):
    b, l, d = x.shape
    lc = ctx.shape[1]
    s = lc + l
    depth = w_ada.shape[0]
    rows = l // GRID_W
    tm = min(256, lc)
    assert lc % tm == 0 and l % tm == 0 and l % GRID_W == 0
    nct = lc // tm
    rb = 4 if rows % 4 == 0 else 1
    tc = min(32, lc)
    tm_moe = _largest_tile(s, 768, 2 * SUBLANES)

    xa = jnp.concatenate([ctx, x], axis=1)

    n_mod = -(-(b + 1) // SUBLANES) * SUBLANES
    cvec = jnp.concatenate([c, c_ctx[None, :], jnp.zeros((n_mod - b - 1, d), F32)], axis=0)
    mod_all = _modulation(cvec, w_ada, b_ada)

    cos, sin_lo, sin_hi = _rope_tables(l, lc)
    head_ones = jnp.asarray(np.kron(np.eye(NA_HEADS), np.ones((HEAD_DIM, HEAD_DIM))), dtype=BF16)
    csl, csc, c64, s64 = _fnet_constants(l, lc)
    ident = jnp.eye(tm, dtype=BF16)
    w_router_t = w_router.T

    for layer in range(depth):
        mod = mod_all[layer]
        modtab = jnp.stack([jnp.broadcast_to(mod[b].reshape(1, 6, d), (b, 6, d)),
                            mod[:b].reshape(b, 6, d)], axis=1)
        qg = jnp.tile(q_norm_g[layer] * ATTN_SCALE, NA_HEADS).reshape(1, NA_WIDTH)
        kg = jnp.tile(k_norm_g[layer], NA_HEADS).reshape(1, NA_WIDTH)
        q, k, v, us, uf = _in_projection(xa, modtab, norm1_g[layer].reshape(1, d), w_in[layer].astype(BF16),
                                         qg, kg, head_ones, cos, sin_lo, sin_hi, tm, nct)

        bias = _attn_bias_tables(rpb[layer], rows, rb)
        att = _attention(q, k, v, bias, lc, rows, rb)

        bmat, cmat, a_re, a_im = _s5_matrices(s5_lam_re[layer], s5_lam_im[layer], s5_log_step[layer],
                                              s5_b_re[layer], s5_b_im[layer], s5_c_re[layer], s5_c_im[layer], b)
        y = _s5_scan(us.reshape(s * b, S5_WIDTH), bmat, cmat, a_re, a_im, b, lc, tc)

        wbd = jnp.einsum('hde,hg->hdge', w_fnet[layer], jnp.eye(FNET_HEADS, dtype=F32))
        fou = _fourier(uf, csl, csc, c64, s64, wbd.reshape(FNET_WIDTH, FNET_WIDTH).astype(BF16), lc)

        x1, h2, gates = _out_projection(
            xa, att, y.reshape(2, s, b * S5_WIDTH), us, fou, modtab, s5_d[layer].reshape(1, S5_WIDTH),
            w_glu[layer].astype(BF16), b_glu[layer].reshape(1, S5_WIDTH), w_out[layer].astype(BF16),
            norm2_g[layer].reshape(1, d), w_router_t, b_router, ident, tm, nct)

        xa = _experts(h2, gates, x1, modtab, w_gate[layer].astype(BF16), w_up[layer].astype(BF16),
                      w_down[layer].astype(BF16), lc, tm_moe)

    return xa[:, lc:]
```

```python
import functools
import math

import numpy as np
import jax
import jax.numpy as jnp
from jax import lax
from jax.experimental import pallas as pl
from jax.experimental.pallas import tpu as pltpu

F32 = jnp.float32
BF16 = jnp.bfloat16

GRID_W = 64
EPS = 1e-6
HEAD_DIM = 64
NA_WIDTH = 512
NA_HEADS = 8
NA_WIN_H = 8
NA_WIN_W = 16
S5_WIDTH = 256
S5_GROUP = 16
S5_GROUPS = 16
S5_STATE = 64
S5_COLS = S5_GROUPS * S5_STATE
FNET_WIDTH = 256
FNET_HEADS = 4
FNET_HEAD_DIM = 64
IN_COLS = 3 * NA_WIDTH + S5_WIDTH + FNET_WIDTH
ATTN_SCALE = HEAD_DIM ** -0.5
ROPE_BASE = 100.0
ROPE_PAIRS_PER_AXIS = HEAD_DIM // 4
N_EXPERTS = 16
N_EXPERT_GROUPS = 4
EXPERTS_PER_GROUP = 4
EXPERT_FF = 512

LANES = 128
SUBLANES = 8
VMEM_LIMIT = 56 * 1024 * 1024

NEG_BIG = -1e30
NT_DIMS = (((1,), (1,)), ((), ()))


def _split_bf16(x):
    hi = x.astype(BF16)
    lo = (x - hi.astype(F32)).astype(BF16)
    return hi, lo


def _dot_f32(a, b, dims=None):
    a_hi, a_lo = _split_bf16(a)
    b_hi, b_lo = _split_bf16(b)
    if dims is None:
        dot = functools.partial(jnp.dot, preferred_element_type=F32)
    else:
        dot = functools.partial(lax.dot_general, dimension_numbers=dims, preferred_element_type=F32)
    return dot(a_hi, b_hi) + dot(a_hi, b_lo) + dot(a_lo, b_hi)


def _params(*sem):
    return pltpu.CompilerParams(dimension_semantics=sem, vmem_limit_bytes=VMEM_LIMIT)


def _mod_kernel(c_ref, w_ref, b_ref, o_ref):
    c = c_ref[...]
    s = c * jax.nn.sigmoid(c)
    o_ref[0] = _dot_f32(s, w_ref[0]) + b_ref[0]


def _modulation(cvec, w_ada, b_ada):
    depth, d, n = w_ada.shape
    r = cvec.shape[0]
    tn = 768
    return pl.pallas_call(
        _mod_kernel,
        out_shape=jax.ShapeDtypeStruct((depth, r, n), F32),
        grid=(depth, n // tn),
        in_specs=[pl.BlockSpec((r, d), lambda l, j: (0, 0)),
                  pl.BlockSpec((1, d, tn), lambda l, j: (l, 0, j)),
                  pl.BlockSpec((1, 1, tn), lambda l, j: (l, 0, j))],
        out_specs=pl.BlockSpec((1, r, tn), lambda l, j: (l, 0, j)),
        compiler_params=_params("arbitrary", "arbitrary"),
    )(cvec, w_ada, b_ada.reshape(depth, 1, n))


def _inproj_kernel(x_ref, mod_ref, g1_ref, w_ref, qg_ref, kg_ref, gm_ref, cos_ref, sa_ref, sb_ref,
                   q_ref, k_ref, v_ref, us_ref, uf_ref):
    x = x_ref[0]
    mod = mod_ref[0, 0]
    r = lax.rsqrt(jnp.mean(x * x, axis=-1, keepdims=True) + EPS)
    h = (x * r * g1_ref[...]) * (1.0 + mod[1:2]) + mod[0:1]
    proj = jnp.dot(h.astype(BF16), w_ref[...], preferred_element_type=F32)

    cos = cos_ref[...]
    sin_lo = sa_ref[...]
    sin_hi = sb_ref[...]
    gm = gm_ref[...]
    nw = NA_WIDTH

    def head_norm_rope(t, g):
        ss = jnp.dot((t * t).astype(BF16), gm, preferred_element_type=F32)
        tn = t * lax.rsqrt(ss * (1.0 / HEAD_DIM) + EPS) * g
        half = HEAD_DIM // 2
        return (tn * cos + pltpu.roll(tn, nw - half, 1) * sin_lo + pltpu.roll(tn, half, 1) * sin_hi)

    q_ref[0] = head_norm_rope(proj[:, 0:nw], qg_ref[...]).astype(BF16)
    k_ref[0] = head_norm_rope(proj[:, nw:2 * nw], kg_ref[...]).astype(BF16)
    v_ref[0] = proj[:, 2 * nw:3 * nw].astype(BF16)
    us_ref[0] = proj[:, 3 * nw:3 * nw + S5_WIDTH]
    uf_ref[0] = proj[:, 3 * nw + S5_WIDTH:].astype(BF16)


def _in_projection(xa, modtab, g1, w_in_bf, qg, kg, gm, cos, sin_lo, sin_hi, tm, nct):
    b, s, d = xa.shape
    nt = s // tm
    nw = NA_WIDTH
    row = lambda i, bb: (bb, i, 0)
    const2 = lambda i, bb: (0, 0)
    tab = lambda i, bb: (i, 0)
    return pl.pallas_call(
        _inproj_kernel,
        out_shape=(jax.ShapeDtypeStruct((b, s, nw), BF16),
                   jax.ShapeDtypeStruct((b, s, nw), BF16),
                   jax.ShapeDtypeStruct((b, s, nw), BF16),
                   jax.ShapeDtypeStruct((b, s, S5_WIDTH), F32),
                   jax.ShapeDtypeStruct((b, s, FNET_WIDTH), BF16)),
        grid=(nt, b),
        in_specs=[pl.BlockSpec((1, tm, d), row),
                  pl.BlockSpec((1, 1, 6, d), lambda i, bb: (bb, jnp.where(i >= nct, 1, 0), 0, 0)),
                  pl.BlockSpec((1, d), const2),
                  pl.BlockSpec((d, IN_COLS), const2),
                  pl.BlockSpec((1, nw), const2),
                  pl.BlockSpec((1, nw), const2),
                  pl.BlockSpec((nw, nw), const2),
                  pl.BlockSpec((tm, nw), tab),
                  pl.BlockSpec((tm, nw), tab),
                  pl.BlockSpec((tm, nw), tab)],
        out_specs=(pl.BlockSpec((1, tm, nw), row),
                   pl.BlockSpec((1, tm, nw), row),
                   pl.BlockSpec((1, tm, nw), row),
                   pl.BlockSpec((1, tm, S5_WIDTH), row),
                   pl.BlockSpec((1, tm, FNET_WIDTH), row)),
        compiler_params=_params("arbitrary", "arbitrary"),
    )(xa, modtab, g1, w_in_bf, qg, kg, gm, cos, sin_lo, sin_hi)


def _attn_blocks(rows, rb):
    kh = min(NA_WIN_H, rows)
    kr = min(rb + kh - 1, rows)
    blocks, classes = [], {}
    for r0 in range(0, rows, rb):
        ks = min(max(r0 - kh // 2, 0), rows - kr)
        starts = tuple(min(max(r - kh // 2, 0), rows - kh) - ks for r in range(r0, r0 + rb))
        key = (r0 - ks, starts)
        cls = classes.setdefault(key, len(classes))
        blocks.append((r0, ks, cls))
    return kh, kr, blocks, list(classes.keys())


def _attn_bias_tables(rpb, rows, rb):
    kh, kr, blocks, class_keys = _attn_blocks(rows, rb)
    w = GRID_W
    col = np.arange(w)
    col_start = np.clip(col - NA_WIN_W // 2, 0, w - NA_WIN_W)
    n_row, n_col = 2 * NA_WIN_H - 1, 2 * NA_WIN_W - 1
    col_off = np.clip(col[None, :] - col[:, None] + (NA_WIN_W - 1), 0, n_col - 1)
    col_pick = jnp.asarray(np.eye(n_col)[col_off].reshape(w * w, n_col), dtype=F32)
    col_valid = (col[None, :] >= col_start[:, None]) & (col[None, :] < col_start[:, None] + NA_WIN_W)
    tabs = []
    for q_minus_ks, starts in class_keys:
        rq = np.arange(rb)[:, None]
        i = np.arange(kr)[None, :]
        rs = np.asarray(starts)[:, None]
        row_valid = (i >= rs) & (i < rs + kh)
        row_off = np.clip(i - (q_minus_ks + rq) + (NA_WIN_H - 1), 0, n_row - 1)
        row_pick = jnp.asarray(np.eye(n_row)[row_off].reshape(rb * kr, n_row), dtype=F32)
        bias = jnp.einsum('hyx,ny,mx->hnm', rpb.astype(F32), row_pick, col_pick,
                          precision=lax.Precision.HIGHEST)
        bias = bias.reshape(-1, rb, kr, w, w).transpose(0, 1, 3, 2, 4)
        valid = row_valid[:, None, :, None] & col_valid[None, :, None, :]
        tabs.append(jnp.where(valid[None], bias, NEG_BIG).reshape(-1, rb * w, kr * w))
    return jnp.stack(tabs, axis=0)


def _attn_kernel(q_ref, k_ref, v_ref, bias_ref, o_ref, *, lc, rb, kr, blocks):
    lane = lax.broadcasted_iota(jnp.int32, (1, LANES), 1)
    first_head = lane < HEAD_DIM

    def stack_heads(qb):
        z = jnp.zeros_like(qb)
        return jnp.concatenate([jnp.where(first_head, qb, z), jnp.where(first_head, z, qb)], axis=0)

    def unstack_heads(o, n):
        return jnp.where(first_head, o[:n], o[n:])

    kc = k_ref[0, 0:lc, :]
    vc = v_ref[0, 0:lc, :]

    s = lax.dot_general(stack_heads(q_ref[0, 0:lc, :]), kc, NT_DIMS, preferred_element_type=F32)
    m = jnp.max(s, axis=-1, keepdims=True)
    p = jnp.exp(s - m)
    den = jnp.sum(p, axis=-1, keepdims=True)
    o = jnp.dot(p.astype(BF16), vc, preferred_element_type=F32) / den
    o_ref[0, 0:lc, :] = unstack_heads(o, lc).astype(BF16)

    nq = rb * GRID_W
    nk = kr * GRID_W
    for r0, ks, cls in blocks:
        q0 = lc + r0 * GRID_W
        k0 = lc + ks * GRID_W
        q2 = stack_heads(q_ref[0, q0:q0 + nq, :])
        k_loc = k_ref[0, k0:k0 + nk, :]
        v_loc = v_ref[0, k0:k0 + nk, :]
        s_loc = lax.dot_general(q2, k_loc, NT_DIMS, preferred_element_type=F32)
        s_loc = s_loc + bias_ref[cls].reshape(2 * nq, nk)
        s_ctx = lax.dot_general(q2, kc, NT_DIMS, preferred_element_type=F32)
        m = jnp.maximum(jnp.max(s_loc, axis=-1, keepdims=True), jnp.max(s_ctx, axis=-1, keepdims=True))
        p_loc = jnp.exp(s_loc - m)
        p_ctx = jnp.exp(s_ctx - m)
        den = jnp.sum(p_loc, axis=-1, keepdims=True) + jnp.sum(p_ctx, axis=-1, keepdims=True)
        o = (jnp.dot(p_loc.astype(BF16), v_loc, preferred_element_type=F32)
             + jnp.dot(p_ctx.astype(BF16), vc, preferred_element_type=F32)) / den
        o_ref[0, q0:q0 + nq, :] = unstack_heads(o, nq).astype(BF16)


def _attention(q, k, v, bias, lc, rows, rb):
    b, s, nw = q.shape
    _, kr, blocks, _ = _attn_blocks(rows, rb)
    ncls = bias.shape[0]
    spec = pl.BlockSpec((1, s, LANES), lambda bb, p: (bb, 0, p))
    return pl.pallas_call(
        functools.partial(_attn_kernel, lc=lc, rb=rb, kr=kr, blocks=blocks),
        out_shape=jax.ShapeDtypeStruct((b, s, nw), BF16),
        grid=(b, nw // LANES),
        in_specs=[spec, spec, spec,
                  pl.BlockSpec((ncls, 2, rb * GRID_W, kr * GRID_W), lambda bb, p: (0, p, 0, 0))],
        out_specs=spec,
        compiler_params=_params("arbitrary", "arbitrary"),
    )(q, k, v, bias)


def _s5_matrices(lam_re, lam_im, log_step, b_re, b_im, c_re, c_im, batch):
    lam_re = lam_re.astype(F32)
    lam_im = lam_im.astype(F32)
    step = jnp.exp(log_step.astype(F32))[..., None]
    mag = jnp.exp(lam_re * step)
    a_re = mag * jnp.cos(lam_im * step)
    a_im = mag * jnp.sin(lam_im * step)
    den = lam_re * lam_re + lam_im * lam_im
    co_re = ((a_re - 1.0) * lam_re + a_im * lam_im) / den
    co_im = (a_im * lam_re - (a_re - 1.0) * lam_im) / den
    b_re = b_re.astype(F32)
    b_im = b_im.astype(F32)
    bb_re = co_re[..., None] * b_re - co_im[..., None] * b_im
    bb_im = co_re[..., None] * b_im + co_im[..., None] * b_re
    eye = jnp.eye(S5_GROUPS, dtype=F32)
    to_in = lambda t: jnp.einsum('dgpc,gh->dgchp', t, eye).reshape(2, S5_WIDTH, S5_COLS)
    bmat = jnp.concatenate([to_in(bb_re), to_in(bb_im)], axis=-1)
    to_out = lambda t: jnp.einsum('dgcp,gh->dgphc', t.astype(F32), eye).reshape(2, S5_COLS, S5_WIDTH)
    cmat = jnp.concatenate([to_out(c_re), -to_out(c_im)], axis=1)
    bc = lambda t: jnp.broadcast_to(t.reshape(2, 1, S5_COLS), (2, batch, S5_COLS))
    return bmat.astype(BF16), cmat.astype(BF16), bc(a_re), bc(a_im)


def _s5_direction(u_ref, bm_ref, cm_ref, ar_ref, ai_ref, y_ref, hbuf, st_re, st_im, tc, batch, reverse):
    u = jnp.swapaxes(u_ref[...], 0, 1).reshape(tc * batch, S5_WIDTH)
    hbuf[...] = jnp.dot(u.astype(BF16), bm_ref[0], preferred_element_type=F32)

    ncol = 2 * LANES
    for cg in range(S5_COLS // ncol):
        re = slice(cg * ncol, (cg + 1) * ncol)
        im = slice(S5_COLS + cg * ncol, S5_COLS + (cg + 1) * ncol)
        h_re = st_re[:, re]
        h_im = st_im[:, re]
        for tt in range(tc):
            t = tc - 1 - tt if reverse else tt
            rows = slice(t * batch, (t + 1) * batch)
            a_re = ar_ref[0, :, re]
            a_im = ai_ref[0, :, re]
            n_re = a_re * h_re - a_im * h_im + hbuf[rows, re]
            n_im = a_re * h_im + a_im * h_re + hbuf[rows, im]
            hbuf[rows, re] = n_re
            hbuf[rows, im] = n_im
            h_re, h_im = n_re, n_im
        st_re[:, re] = h_re
        st_im[:, re] = h_im

    y = jnp.dot(hbuf[...].astype(BF16), cm_ref[0], preferred_element_type=F32)
    y_ref[...] = jnp.swapaxes(y.reshape(tc, batch, S5_WIDTH), 0, 1)


def _s5_kernel(uf_ref, ub_ref, bmf_ref, bmb_ref, cmf_ref, cmb_ref, arf_ref, arb_ref, aif_ref, aib_ref,
               yf_ref, yb_ref, hbuf_f, hbuf_b, sf_re, sf_im, sb_re, sb_im, *, tc, batch):
    @pl.when(pl.program_id(0) == 0)
    def _():
        for st in (sf_re, sf_im, sb_re, sb_im):
            st[...] = jnp.zeros_like(st)

    _s5_direction(uf_ref, bmf_ref, cmf_ref, arf_ref, aif_ref, yf_ref, hbuf_f, sf_re, sf_im, tc, batch, False)
    _s5_direction(ub_ref, bmb_ref, cmb_ref, arb_ref, aib_ref, yb_ref, hbuf_b, sb_re, sb_im, tc, batch, True)


def _s5_scan(us, bmat, cmat, a_re, a_im, lc, tc):
    batch, s, w = us.shape
    nch = s // tc
    nch_c = lc // tc
    fwd = lambda j: (0, j, 0)
    bwd = lambda j: (0, jnp.where(j < nch_c, nch_c - 1 - j, nch + nch_c - 1 - j), 0)
    dir_f = lambda j: (0, 0, 0)
    dir_b = lambda j: (1, 0, 0)
    hbuf = pltpu.VMEM((tc * batch, 2 * S5_COLS), F32)
    state = pltpu.VMEM((batch, S5_COLS), F32)
    out = jax.ShapeDtypeStruct((batch, s, w), F32)
    return pl.pallas_call(
        functools.partial(_s5_kernel, tc=tc, batch=batch),
        out_shape=(out, out),
        grid=(nch,),
        in_specs=[pl.BlockSpec((batch, tc, w), fwd),
                  pl.BlockSpec((batch, tc, w), bwd),
                  pl.BlockSpec((1, w, 2 * S5_COLS), dir_f),
                  pl.BlockSpec((1, w, 2 * S5_COLS), dir_b),
                  pl.BlockSpec((1, 2 * S5_COLS, w), dir_f),
                  pl.BlockSpec((1, 2 * S5_COLS, w), dir_b),
                  pl.BlockSpec((1, batch, S5_COLS), dir_f),
                  pl.BlockSpec((1, batch, S5_COLS), dir_b),
                  pl.BlockSpec((1, batch, S5_COLS), dir_f),
                  pl.BlockSpec((1, batch, S5_COLS), dir_b)],
        out_specs=(pl.BlockSpec((batch, tc, w), fwd), pl.BlockSpec((batch, tc, w), bwd)),
        scratch_shapes=[hbuf, hbuf, state, state, state, state],
        compiler_params=_params("arbitrary"),
    )(us, us, bmat, bmat, cmat, cmat, a_re, a_re, a_im, a_im)


def _dft_cos_sin(n):
    k = np.arange(n, dtype=np.int64)
    ang = 2.0 * np.pi * ((k[:, None] * k[None, :]) % n).astype(np.float64) / n
    return np.cos(ang), np.sin(ang)


def _fnet_constants(l, lc):
    def seq_matrix(n):
        c, s = _dft_cos_sin(n)
        scale = 1.0 / math.sqrt(n * FNET_HEAD_DIM)
        return jnp.asarray(np.concatenate([c, -s], axis=1) * scale, dtype=F32).astype(BF16)
    c64, s64 = _dft_cos_sin(FNET_HEAD_DIM)
    eye = np.eye(FNET_HEADS)
    return (seq_matrix(l), seq_matrix(lc),
            jnp.asarray(np.kron(eye, c64), dtype=F32).astype(BF16),
            jnp.asarray(np.kron(eye, s64), dtype=F32).astype(BF16))


def _fnet_kernel(u_ref, csl_ref, csc_ref, c64_ref, s64_ref, w_ref, o_ref, tl_ref, tc_ref, *, lc, l):
    i = pl.program_id(1)
    w = w_ref[...]

    @pl.when(i == 0)
    def _():
        u = u_ref[0]
        t_cos = jnp.dot(u, c64_ref[...], preferred_element_type=F32).astype(BF16)
        t_sin = jnp.dot(u, s64_ref[...], preferred_element_type=F32).astype(BF16)
        tc_ref[0:lc, :] = t_cos[0:lc]
        tc_ref[lc:2 * lc, :] = t_sin[0:lc]
        tl_ref[0:l, :] = t_cos[lc:]
        tl_ref[l:2 * l, :] = t_sin[lc:]
        z = jnp.dot(csc_ref[...], tc_ref[...], preferred_element_type=F32)
        o_ref[0] = jnp.dot(z.astype(BF16), w, preferred_element_type=F32).astype(BF16)

    @pl.when(i > 0)
    def _():
        z = jnp.dot(csl_ref[...], tl_ref[...], preferred_element_type=F32)
        o_ref[0] = jnp.dot(z.astype(BF16), w, preferred_element_type=F32).astype(BF16)


def _fourier(uf, csl, csc, c64, s64, wbd, lc):
    b, s, w = uf.shape
    l = s - lc
    tmf = lc
    nt = 1 + l // tmf
    const = lambda bb, i: (0, 0)
    return pl.pallas_call(
        functools.partial(_fnet_kernel, lc=lc, l=l),
        out_shape=jax.ShapeDtypeStruct((b, s, w), BF16),
        grid=(b, nt),
        in_specs=[pl.BlockSpec((1, s, w), lambda bb, i: (bb, 0, 0)),
                  pl.BlockSpec((tmf, 2 * l), lambda bb, i: (jnp.maximum(i - 1, 0), 0)),
                  pl.BlockSpec((lc, 2 * lc), const),
                  pl.BlockSpec((w, w), const),
                  pl.BlockSpec((w, w), const),
                  pl.BlockSpec((w, w), const)],
        out_specs=pl.BlockSpec((1, tmf, w), lambda bb, i: (bb, i, 0)),
        scratch_shapes=[pltpu.VMEM((2 * l, w), BF16), pltpu.VMEM((2 * lc, w), BF16)],
        compiler_params=_params("arbitrary", "arbitrary"),
    )(uf, csl, csc, c64, s64, wbd)


def _gelu_tanh(x):
    return 0.5 * x * (1.0 + jnp.tanh(math.sqrt(2.0 / math.pi) * (x + 0.044715 * (x * x * x))))


def _route(logits_t, b_router_ref):
    e_all = range(N_EXPERTS)
    aff = [jax.nn.sigmoid(logits_t[e:e + 1, :]) for e in e_all]
    sel = [aff[e] + b_router_ref[e] for e in e_all]
    epg = EXPERTS_PER_GROUP
    keep = []
    for e in e_all:
        g0 = (e // epg) * epg
        rank = jnp.zeros_like(sel[e], dtype=jnp.int32)
        for o in range(g0, g0 + epg):
            if o == e:
                continue
            ahead = (sel[o] >= sel[e]) if o < e else (sel[o] > sel[e])
            rank = rank + ahead.astype(jnp.int32)
        keep.append(rank < 2)
    score = []
    for g in range(N_EXPERT_GROUPS):
        s0, s1, s2, s3 = sel[g * epg:(g + 1) * epg]
        hi01, lo01 = jnp.maximum(s0, s1), jnp.minimum(s0, s1)
        hi23, lo23 = jnp.maximum(s2, s3), jnp.minimum(s2, s3)
        top1 = jnp.maximum(hi01, hi23)
        top2 = jnp.maximum(jnp.minimum(hi01, hi23), jnp.maximum(lo01, lo23))
        score.append(top1 + top2)
    chosen = []
    for g in range(N_EXPERT_GROUPS):
        best = None
        for o in range(N_EXPERT_GROUPS):
            if o == g:
                continue
            cond = (score[g] > score[o]) if o < g else (score[g] >= score[o])
            best = cond if best is None else (best & cond)
        chosen.append(best)
    picked = [jnp.where(chosen[e // epg] & keep[e], aff[e], 0.0) for e in e_all]
    total = picked[0]
    for e in range(1, N_EXPERTS):
        total = total + picked[e]
    return [p / total for p in picked]


def _outproj_kernel(x_ref, att_ref, yf_ref, yb_ref, us_ref, fou_ref, mod_ref, sd_ref, wglu_ref, bglu_ref,
                    wout_ref, g2_ref, wr_ref, br_ref, ident_ref,
                    x1_ref, h2_ref, gate_ref):
    mod = mod_ref[0, 0]
    y = sd_ref[...] * us_ref[0] + yf_ref[0] + yb_ref[0]
    g = _gelu_tanh(y)
    ssm = g * jax.nn.sigmoid(jnp.dot(g.astype(BF16), wglu_ref[...], preferred_element_type=F32)
                             + bglu_ref[...])
    a0, a1 = NA_WIDTH, NA_WIDTH + S5_WIDTH
    o = (jnp.dot(att_ref[0], wout_ref[0:a0, :], preferred_element_type=F32)
         + jnp.dot(ssm.astype(BF16), wout_ref[a0:a1, :], preferred_element_type=F32)
         + jnp.dot(fou_ref[0], wout_ref[a1:, :], preferred_element_type=F32))
    x1 = x_ref[0] + mod[2:3] * o
    x1_ref[0] = x1
    r = lax.rsqrt(jnp.mean(x1 * x1, axis=-1, keepdims=True) + EPS)
    h2 = (x1 * r * g2_ref[...]) * (1.0 + mod[4:5]) + mod[3:4]
    h2_ref[0] = h2.astype(BF16)

    logits_t = _dot_f32(wr_ref[...], h2, NT_DIMS)
    gates = _route(logits_t, br_ref)
    tm = h2.shape[0]
    sub = lax.broadcasted_iota(jnp.int32, (LANES, tm), 0)
    gt = jnp.zeros((LANES, tm), F32)
    for e in range(N_EXPERTS):
        gt = jnp.where(sub == e, gates[e], gt)
    p0 = gt.astype(BF16)
    r1 = gt - p0.astype(F32)
    p1 = r1.astype(BF16)
    p2 = (r1 - p1.astype(F32)).astype(BF16)
    ident = ident_ref[...]
    nt = functools.partial(lax.dot_general, dimension_numbers=NT_DIMS, preferred_element_type=F32)
    gate_ref[0] = nt(ident, p0) + nt(ident, p1) + nt(ident, p2)


def _out_projection(xa, att, y_fwd, y_bwd, us, fou, modtab, s5_d, w_glu_bf, b_glu, w_out_bf, g2, w_router_t, b_router,
                    ident, tm, nct):
    b, s, d = xa.shape
    nt = s // tm
    row = lambda i, bb: (bb, i, 0)
    const2 = lambda i, bb: (0, 0)
    return pl.pallas_call(
        _outproj_kernel,
        out_shape=(jax.ShapeDtypeStruct((b, s, d), F32),
                   jax.ShapeDtypeStruct((b, s, d), BF16),
                   jax.ShapeDtypeStruct((b, s, LANES), F32)),
        grid=(nt, b),
        in_specs=[pl.BlockSpec((1, tm, d), row),
                  pl.BlockSpec((1, tm, NA_WIDTH), row),
                  pl.BlockSpec((1, tm, S5_WIDTH), row),
                  pl.BlockSpec((1, tm, S5_WIDTH), row),
                  pl.BlockSpec((1, tm, S5_WIDTH), row),
                  pl.BlockSpec((1, tm, FNET_WIDTH), row),
                  pl.BlockSpec((1, 1, 6, d), lambda i, bb: (bb, jnp.where(i >= nct, 1, 0), 0, 0)),
                  pl.BlockSpec((1, S5_WIDTH), const2),
                  pl.BlockSpec((S5_WIDTH, S5_WIDTH), const2),
                  pl.BlockSpec((1, S5_WIDTH), const2),
                  pl.BlockSpec((d, d), const2),
                  pl.BlockSpec((1, d), const2),
                  pl.BlockSpec((N_EXPERTS, d), const2),
                  pl.BlockSpec(memory_space=pltpu.SMEM),
                  pl.BlockSpec((tm, tm), const2)],
        out_specs=(pl.BlockSpec((1, tm, d), row),
                   pl.BlockSpec((1, tm, d), row),
                   pl.BlockSpec((1, tm, LANES), row)),
        compiler_params=_params("arbitrary", "arbitrary"),
    )(xa, att, y_fwd, y_bwd, us, fou, modtab, s5_d, w_glu_bf, b_glu, w_out_bf, g2, w_router_t, b_router, ident)


def _moe_kernel(h_ref, gate_ref, x1_ref, mod_ref, wg_ref, wu_ref, wd_ref, o_ref, acc_ref, *, lc, tm):
    s = pl.program_id(1)
    e = pl.program_id(2)

    @pl.when(e == 0)
    def _():
        acc_ref[...] = jnp.zeros_like(acc_ref)

    h = h_ref[0]
    lane = lax.broadcasted_iota(jnp.int32, (1, LANES), 1)
    ge = jnp.sum(jnp.where(lane == e, gate_ref[0], 0.0), axis=-1, keepdims=True)
    gp = jnp.dot(h, wg_ref[0], preferred_element_type=F32)
    up = jnp.dot(h, wu_ref[0], preferred_element_type=F32)
    he = (gp * jax.nn.sigmoid(gp)) * up * ge
    acc_ref[...] += jnp.dot(he.astype(BF16), wd_ref[0], preferred_element_type=F32)

    @pl.when(e == N_EXPERTS - 1)
    def _():
        rowi = s * tm + lax.broadcasted_iota(jnp.int32, (tm, 1), 0)
        m5 = jnp.where(rowi < lc, mod_ref[0, 0, 5:6, :], mod_ref[0, 1, 5:6, :])
        o_ref[0] = x1_ref[0] + m5 * acc_ref[...]


def _experts(h2, gates, x1, modtab, wg_bf, wu_bf, wd_bf, lc, tm):
    b, s, d = x1.shape
    ns = s // tm
    row = lambda bb, i, e: (bb, i, 0)
    per_e = lambda bb, i, e: (e, 0, 0)
    return pl.pallas_call(
        functools.partial(_moe_kernel, lc=lc, tm=tm),
        out_shape=jax.ShapeDtypeStruct((b, s, d), F32),
        grid=(b, ns, N_EXPERTS),
        in_specs=[pl.BlockSpec((1, tm, d), row),
                  pl.BlockSpec((1, tm, LANES), row),
                  pl.BlockSpec((1, tm, d), row),
                  pl.BlockSpec((1, 2, 6, d), lambda bb, i, e: (bb, 0, 0, 0)),
                  pl.BlockSpec((1, d, EXPERT_FF), per_e),
                  pl.BlockSpec((1, d, EXPERT_FF), per_e),
                  pl.BlockSpec((1, EXPERT_FF, d), per_e)],
        out_specs=pl.BlockSpec((1, tm, d), row),
        scratch_shapes=[pltpu.VMEM((tm, d), F32)],
        compiler_params=_params("arbitrary", "arbitrary", "arbitrary"),
    )(h2, gates, x1, modtab, wg_bf, wu_bf, wd_bf)


def _rope_tables(l, lc):
    t = jnp.arange(l, dtype=jnp.int32)
    row = (t // GRID_W).astype(F32)
    col = (t % GRID_W).astype(F32)
    inv_freq = ROPE_BASE ** (-jnp.arange(ROPE_PAIRS_PER_AXIS, dtype=F32) / ROPE_PAIRS_PER_AXIS)
    ang = jnp.concatenate([row[:, None] * inv_freq, col[:, None] * inv_freq], axis=-1)
    cos, sin = jnp.cos(ang), jnp.sin(ang)
    zero = jnp.zeros_like(sin)
    cos_h = jnp.concatenate([cos, cos], axis=-1)
    sin_lo = jnp.concatenate([-sin, zero], axis=-1)
    sin_hi = jnp.concatenate([zero, sin], axis=-1)
    def full(tab, ctx_value):
        tab = jnp.tile(tab, (1, NA_HEADS))
        return jnp.concatenate([jnp.full((lc, NA_WIDTH), ctx_value, F32), tab], axis=0)
    return full(cos_h, 1.0), full(sin_lo, 0.0), full(sin_hi, 0.0)


def _largest_tile(n, limit, multiple):
    best = multiple
    for t in range(multiple, limit + 1, multiple):
        if n % t == 0:
            best = t
    return best


def kernel(x, c, ctx, c_ctx, w_ada, b_ada, norm1_g, norm2_g, w_in, q_norm_g, k_norm_g, rpb, s5_lam_re, s5_lam_im, s5_log_step, s5_b_re, s5_b_im, s5_c_re, s5_c_im, s5_d, w_glu, b_glu, w_fnet, w_out, w_router, b_router, w_gate, w_up, w_down):
    b, l, d = x.shape
    lc = ctx.shape[1]
    s = lc + l
    depth = w_ada.shape[0]
    rows = l // GRID_W
    tm = min(256, lc)
    assert lc % tm == 0 and l % tm == 0 and l % GRID_W == 0
    nct = lc // tm
    rb = 4 if rows % 4 == 0 else 1
    tc = min(64, lc)
    tm_moe = _largest_tile(s, 768, 2 * SUBLANES)

    xa = jnp.concatenate([ctx, x], axis=1)

    n_mod = -(-(b + 1) // SUBLANES) * SUBLANES
    cvec = jnp.concatenate([c, c_ctx[None, :], jnp.zeros((n_mod - b - 1, d), F32)], axis=0)
    mod_all = _modulation(cvec, w_ada, b_ada)

    cos, sin_lo, sin_hi = _rope_tables(l, lc)
    head_ones = jnp.asarray(np.kron(np.eye(NA_HEADS), np.ones((HEAD_DIM, HEAD_DIM))), dtype=BF16)
    csl, csc, c64, s64 = _fnet_constants(l, lc)
    ident = jnp.eye(tm, dtype=BF16)
    w_router_t = w_router.T

    for layer in range(depth):
        mod = mod_all[layer]
        modtab = jnp.stack([jnp.broadcast_to(mod[b].reshape(1, 6, d), (b, 6, d)),
                            mod[:b].reshape(b, 6, d)], axis=1)
        qg = jnp.tile(q_norm_g[layer] * ATTN_SCALE, NA_HEADS).reshape(1, NA_WIDTH)
        kg = jnp.tile(k_norm_g[layer], NA_HEADS).reshape(1, NA_WIDTH)
        q, k, v, us, uf = _in_projection(xa, modtab, norm1_g[layer].reshape(1, d), w_in[layer].astype(BF16),
                                         qg, kg, head_ones, cos, sin_lo, sin_hi, tm, nct)

        bias = _attn_bias_tables(rpb[layer], rows, rb)
        att = _attention(q, k, v, bias, lc, rows, rb)

        bmat, cmat, a_re, a_im = _s5_matrices(s5_lam_re[layer], s5_lam_im[layer], s5_log_step[layer],
                                              s5_b_re[layer], s5_b_im[layer], s5_c_re[layer], s5_c_im[layer], b)
        y_fwd, y_bwd = _s5_scan(us, bmat, cmat, a_re, a_im, lc, tc)

        wbd = jnp.einsum('hde,hg->hdge', w_fnet[layer], jnp.eye(FNET_HEADS, dtype=F32))
        fou = _fourier(uf, csl, csc, c64, s64, wbd.reshape(FNET_WIDTH, FNET_WIDTH).astype(BF16), lc)

        x1, h2, gates = _out_projection(
            xa, att, y_fwd, y_bwd, us, fou, modtab, s5_d[layer].reshape(1, S5_WIDTH),
            w_glu[layer].astype(BF16), b_glu[layer].reshape(1, S5_WIDTH), w_out[layer].astype(BF16),
            norm2_g[layer].reshape(1, d), w_router_t, b_router, ident, tm, nct)

        xa = _experts(h2, gates, x1, modtab, w_gate[layer].astype(BF16), w_up[layer].astype(BF16),
                      w_down[layer].astype(BF16), lc, tm_moe)

    return xa[:, lc:]
```

```python
import functools
import math

import numpy as np
import jax
import jax.numpy as jnp
from jax import lax
from jax.experimental import pallas as pl
from jax.experimental.pallas import tpu as pltpu

F32 = jnp.float32
BF16 = jnp.bfloat16

GRID_W = 64
EPS = 1e-6
HEAD_DIM = 64
NA_WIDTH = 512
NA_HEADS = 8
NA_WIN_H = 8
NA_WIN_W = 16
S5_WIDTH = 256
S5_GROUP = 16
S5_GROUPS = 16
S5_STATE = 64
S5_COLS = S5_GROUPS * S5_STATE
FNET_WIDTH = 256
FNET_HEADS = 4
FNET_HEAD_DIM = 64
IN_COLS = 3 * NA_WIDTH + S5_WIDTH + FNET_WIDTH
ATTN_SCALE = HEAD_DIM ** -0.5
ROPE_BASE = 100.0
ROPE_PAIRS_PER_AXIS = HEAD_DIM // 4
N_EXPERTS = 16
GROUP_ROW = N_EXPERTS
N_EXPERT_GROUPS = 4
EXPERTS_PER_GROUP = 4
EXPERT_FF = 512

LANES = 128
SUBLANES = 8
VMEM_LIMIT = 56 * 1024 * 1024

NEG_BIG = -1e30
NT_DIMS = (((1,), (1,)), ((), ()))


def _split_bf16(x):
    hi = x.astype(BF16)
    lo = (x - hi.astype(F32)).astype(BF16)
    return hi, lo


def _dot_f32(a, b, dims=None):
    a_hi, a_lo = _split_bf16(a)
    b_hi, b_lo = _split_bf16(b)
    if dims is None:
        dot = functools.partial(jnp.dot, preferred_element_type=F32)
    else:
        dot = functools.partial(lax.dot_general, dimension_numbers=dims, preferred_element_type=F32)
    return dot(a_hi, b_hi) + dot(a_hi, b_lo) + dot(a_lo, b_hi)


def _params(*sem):
    return pltpu.CompilerParams(dimension_semantics=sem, vmem_limit_bytes=VMEM_LIMIT)


def _mod_kernel(c_ref, w_ref, b_ref, o_ref):
    c = c_ref[...]
    s = c * jax.nn.sigmoid(c)
    o_ref[0] = _dot_f32(s, w_ref[0]) + b_ref[0]


def _modulation(cvec, w_ada, b_ada):
    depth, d, n = w_ada.shape
    r = cvec.shape[0]
    tn = 768
    return pl.pallas_call(
        _mod_kernel,
        out_shape=jax.ShapeDtypeStruct((depth, r, n), F32),
        grid=(depth, n // tn),
        in_specs=[pl.BlockSpec((r, d), lambda l, j: (0, 0)),
                  pl.BlockSpec((1, d, tn), lambda l, j: (l, 0, j)),
                  pl.BlockSpec((1, 1, tn), lambda l, j: (l, 0, j))],
        out_specs=pl.BlockSpec((1, r, tn), lambda l, j: (l, 0, j)),
        compiler_params=_params("arbitrary", "arbitrary"),
    )(cvec, w_ada, b_ada.reshape(depth, 1, n))


def _inproj_kernel(x_ref, mod_ref, g1_ref, w_ref, qg_ref, kg_ref, gm_ref, cos_ref, sa_ref, sb_ref,
                   q_ref, k_ref, v_ref, us_ref, uf_ref):
    x = x_ref[0]
    mod = mod_ref[0, 0]
    r = lax.rsqrt(jnp.mean(x * x, axis=-1, keepdims=True) + EPS)
    h = (x * r * g1_ref[...]) * (1.0 + mod[1:2]) + mod[0:1]
    proj = jnp.dot(h.astype(BF16), w_ref[...], preferred_element_type=F32)

    cos = cos_ref[...]
    sin_lo = sa_ref[...]
    sin_hi = sb_ref[...]
    gm = gm_ref[...]
    nw = NA_WIDTH

    def head_norm_rope(t, g):
        ss = jnp.dot((t * t).astype(BF16), gm, preferred_element_type=F32)
        tn = t * lax.rsqrt(ss * (1.0 / HEAD_DIM) + EPS) * g
        half = HEAD_DIM // 2
        return (tn * cos + pltpu.roll(tn, nw - half, 1) * sin_lo + pltpu.roll(tn, half, 1) * sin_hi)

    q_ref[0] = head_norm_rope(proj[:, 0:nw], qg_ref[...]).astype(BF16)
    k_ref[0] = head_norm_rope(proj[:, nw:2 * nw], kg_ref[...]).astype(BF16)
    v_ref[0] = proj[:, 2 * nw:3 * nw].astype(BF16)
    us_ref[0] = proj[:, 3 * nw:3 * nw + S5_WIDTH]
    uf_ref[0] = proj[:, 3 * nw + S5_WIDTH:].astype(BF16)


def _in_projection(xa, modtab, g1, w_in_bf, qg, kg, gm, cos, sin_lo, sin_hi, tm, nct):
    b, s, d = xa.shape
    nt = s // tm
    nw = NA_WIDTH
    row = lambda i, bb: (bb, i, 0)
    const2 = lambda i, bb: (0, 0)
    tab = lambda i, bb: (i, 0)
    return pl.pallas_call(
        _inproj_kernel,
        out_shape=(jax.ShapeDtypeStruct((b, s, nw), BF16),
                   jax.ShapeDtypeStruct((b, s, nw), BF16),
                   jax.ShapeDtypeStruct((b, s, nw), BF16),
                   jax.ShapeDtypeStruct((b, s, S5_WIDTH), F32),
                   jax.ShapeDtypeStruct((b, s, FNET_WIDTH), BF16)),
        grid=(nt, b),
        in_specs=[pl.BlockSpec((1, tm, d), row),
                  pl.BlockSpec((1, 1, 6, d), lambda i, bb: (bb, jnp.where(i >= nct, 1, 0), 0, 0)),
                  pl.BlockSpec((1, d), const2),
                  pl.BlockSpec((d, IN_COLS), const2),
                  pl.BlockSpec((1, nw), const2),
                  pl.BlockSpec((1, nw), const2),
                  pl.BlockSpec((nw, nw), const2),
                  pl.BlockSpec((tm, nw), tab),
                  pl.BlockSpec((tm, nw), tab),
                  pl.BlockSpec((tm, nw), tab)],
        out_specs=(pl.BlockSpec((1, tm, nw), row),
                   pl.BlockSpec((1, tm, nw), row),
                   pl.BlockSpec((1, tm, nw), row),
                   pl.BlockSpec((1, tm, S5_WIDTH), row),
                   pl.BlockSpec((1, tm, FNET_WIDTH), row)),
        compiler_params=_params("arbitrary", "arbitrary"),
    )(xa, modtab, g1, w_in_bf, qg, kg, gm, cos, sin_lo, sin_hi)


def _attn_blocks(rows, rb):
    kh = min(NA_WIN_H, rows)
    kr = min(rb + kh - 1, rows)
    blocks, classes = [], {}
    for r0 in range(0, rows, rb):
        ks = min(max(r0 - kh // 2, 0), rows - kr)
        starts = tuple(min(max(r - kh // 2, 0), rows - kh) - ks for r in range(r0, r0 + rb))
        key = (r0 - ks, starts)
        cls = classes.setdefault(key, len(classes))
        blocks.append((r0, ks, cls))
    return kh, kr, blocks, list(classes.keys())


def _attn_bias_tables(rpb, rows, rb):
    kh, kr, blocks, class_keys = _attn_blocks(rows, rb)
    w = GRID_W
    col = np.arange(w)
    col_start = np.clip(col - NA_WIN_W // 2, 0, w - NA_WIN_W)
    n_row, n_col = 2 * NA_WIN_H - 1, 2 * NA_WIN_W - 1
    col_off = np.clip(col[None, :] - col[:, None] + (NA_WIN_W - 1), 0, n_col - 1)
    col_pick = jnp.asarray(np.eye(n_col)[col_off].reshape(w * w, n_col), dtype=F32)
    col_valid = (col[None, :] >= col_start[:, None]) & (col[None, :] < col_start[:, None] + NA_WIN_W)
    tabs = []
    for q_minus_ks, starts in class_keys:
        rq = np.arange(rb)[:, None]
        i = np.arange(kr)[None, :]
        rs = np.asarray(starts)[:, None]
        row_valid = (i >= rs) & (i < rs + kh)
        row_off = np.clip(i - (q_minus_ks + rq) + (NA_WIN_H - 1), 0, n_row - 1)
        row_pick = jnp.asarray(np.eye(n_row)[row_off].reshape(rb * kr, n_row), dtype=F32)
        bias = jnp.einsum('hyx,ny,mx->hnm', rpb.astype(F32), row_pick, col_pick,
                          precision=lax.Precision.HIGHEST)
        bias = bias.reshape(-1, rb, kr, w, w).transpose(0, 1, 3, 2, 4)
        valid = row_valid[:, None, :, None] & col_valid[None, :, None, :]
        tabs.append(jnp.where(valid[None], bias, NEG_BIG).reshape(-1, rb * w, kr * w))
    return jnp.stack(tabs, axis=0)


def _attn_kernel(q_ref, k_ref, v_ref, bias_ref, o_ref, *, lc, rb, kr, blocks):
    lane = lax.broadcasted_iota(jnp.int32, (1, LANES), 1)
    first_head = lane < HEAD_DIM

    def stack_heads(qb):
        z = jnp.zeros_like(qb)
        return jnp.concatenate([jnp.where(first_head, qb, z), jnp.where(first_head, z, qb)], axis=0)

    def unstack_heads(o, n):
        return jnp.where(first_head, o[:n], o[n:])

    kc = k_ref[0, 0:lc, :]
    vc = v_ref[0, 0:lc, :]

    s = lax.dot_general(stack_heads(q_ref[0, 0:lc, :]), kc, NT_DIMS, preferred_element_type=F32)
    m = jnp.max(s, axis=-1, keepdims=True)
    p = jnp.exp(s - m)
    den = jnp.sum(p, axis=-1, keepdims=True)
    o = jnp.dot(p.astype(BF16), vc, preferred_element_type=F32) / den
    o_ref[0, 0:lc, :] = unstack_heads(o, lc).astype(BF16)

    nq = rb * GRID_W
    nk = kr * GRID_W
    for r0, ks, cls in blocks:
        q0 = lc + r0 * GRID_W
        k0 = lc + ks * GRID_W
        q2 = stack_heads(q_ref[0, q0:q0 + nq, :])
        k_loc = k_ref[0, k0:k0 + nk, :]
        v_loc = v_ref[0, k0:k0 + nk, :]
        s_loc = lax.dot_general(q2, k_loc, NT_DIMS, preferred_element_type=F32)
        s_loc = s_loc + bias_ref[cls].reshape(2 * nq, nk)
        s_ctx = lax.dot_general(q2, kc, NT_DIMS, preferred_element_type=F32)
        m = jnp.maximum(jnp.max(s_loc, axis=-1, keepdims=True), jnp.max(s_ctx, axis=-1, keepdims=True))
        p_loc = jnp.exp(s_loc - m)
        p_ctx = jnp.exp(s_ctx - m)
        den = jnp.sum(p_loc, axis=-1, keepdims=True) + jnp.sum(p_ctx, axis=-1, keepdims=True)
        o = (jnp.dot(p_loc.astype(BF16), v_loc, preferred_element_type=F32)
             + jnp.dot(p_ctx.astype(BF16), vc, preferred_element_type=F32)) / den
        o_ref[0, q0:q0 + nq, :] = unstack_heads(o, nq).astype(BF16)


def _attention(q, k, v, bias, lc, rows, rb):
    b, s, nw = q.shape
    _, kr, blocks, _ = _attn_blocks(rows, rb)
    ncls = bias.shape[0]
    spec = pl.BlockSpec((1, s, LANES), lambda bb, p: (bb, 0, p))
    return pl.pallas_call(
        functools.partial(_attn_kernel, lc=lc, rb=rb, kr=kr, blocks=blocks),
        out_shape=jax.ShapeDtypeStruct((b, s, nw), BF16),
        grid=(b, nw // LANES),
        in_specs=[spec, spec, spec,
                  pl.BlockSpec((ncls, 2, rb * GRID_W, kr * GRID_W), lambda bb, p: (0, p, 0, 0))],
        out_specs=spec,
        compiler_params=_params("arbitrary", "arbitrary"),
    )(q, k, v, bias)


def _s5_matrices(lam_re, lam_im, log_step, b_re, b_im, c_re, c_im, batch):
    lam_re = lam_re.astype(F32)
    lam_im = lam_im.astype(F32)
    step = jnp.exp(log_step.astype(F32))[..., None]
    mag = jnp.exp(lam_re * step)
    a_re = mag * jnp.cos(lam_im * step)
    a_im = mag * jnp.sin(lam_im * step)
    den = lam_re * lam_re + lam_im * lam_im
    co_re = ((a_re - 1.0) * lam_re + a_im * lam_im) / den
    co_im = (a_im * lam_re - (a_re - 1.0) * lam_im) / den
    b_re = b_re.astype(F32)
    b_im = b_im.astype(F32)
    bb_re = co_re[..., None] * b_re - co_im[..., None] * b_im
    bb_im = co_re[..., None] * b_im + co_im[..., None] * b_re
    eye = jnp.eye(S5_GROUPS, dtype=F32)
    to_in = lambda t: jnp.einsum('dgpc,gh->dgchp', t, eye).reshape(2, S5_WIDTH, S5_COLS)
    bmat = jnp.concatenate([to_in(bb_re), to_in(bb_im)], axis=-1)
    to_out = lambda t: jnp.einsum('dgcp,gh->dgphc', t.astype(F32), eye).reshape(2, S5_COLS, S5_WIDTH)
    cmat = jnp.concatenate([to_out(c_re), -to_out(c_im)], axis=1)
    bc = lambda t: jnp.broadcast_to(t.reshape(2, 1, S5_COLS), (2, batch, S5_COLS))
    return bmat.astype(BF16), cmat.astype(BF16), bc(a_re), bc(a_im)


def _s5_direction(u_ref, bm_ref, cm_ref, ar_ref, ai_ref, y_ref, hbuf, st_re, st_im, tc, batch, reverse):
    u = jnp.swapaxes(u_ref[...], 0, 1).reshape(tc * batch, S5_WIDTH)
    hbuf[...] = jnp.dot(u.astype(BF16), bm_ref[0], preferred_element_type=F32)

    ncol = 2 * LANES
    for cg in range(S5_COLS // ncol):
        re = slice(cg * ncol, (cg + 1) * ncol)
        im = slice(S5_COLS + cg * ncol, S5_COLS + (cg + 1) * ncol)
        h_re = st_re[:, re]
        h_im = st_im[:, re]
        for tt in range(tc):
            t = tc - 1 - tt if reverse else tt
            rows = slice(t * batch, (t + 1) * batch)
            a_re = ar_ref[0, :, re]
            a_im = ai_ref[0, :, re]
            n_re = a_re * h_re - a_im * h_im + hbuf[rows, re]
            n_im = a_re * h_im + a_im * h_re + hbuf[rows, im]
            hbuf[rows, re] = n_re
            hbuf[rows, im] = n_im
            h_re, h_im = n_re, n_im
        st_re[:, re] = h_re
        st_im[:, re] = h_im

    y = jnp.dot(hbuf[...].astype(BF16), cm_ref[0], preferred_element_type=F32)
    y_ref[...] = jnp.swapaxes(y.reshape(tc, batch, S5_WIDTH), 0, 1)


def _s5_kernel(uf_ref, ub_ref, bmf_ref, bmb_ref, cmf_ref, cmb_ref, arf_ref, arb_ref, aif_ref, aib_ref,
               yf_ref, yb_ref, hbuf_f, hbuf_b, sf_re, sf_im, sb_re, sb_im, *, tc, batch):
    @pl.when(pl.program_id(0) == 0)
    def _():
        for st in (sf_re, sf_im, sb_re, sb_im):
            st[...] = jnp.zeros_like(st)

    _s5_direction(uf_ref, bmf_ref, cmf_ref, arf_ref, aif_ref, yf_ref, hbuf_f, sf_re, sf_im, tc, batch, False)
    _s5_direction(ub_ref, bmb_ref, cmb_ref, arb_ref, aib_ref, yb_ref, hbuf_b, sb_re, sb_im, tc, batch, True)


def _s5_scan(us, bmat, cmat, a_re, a_im, lc, tc):
    batch, s, w = us.shape
    nch = s // tc
    nch_c = lc // tc
    fwd = lambda j: (0, j, 0)
    bwd = lambda j: (0, jnp.where(j < nch_c, nch_c - 1 - j, nch + nch_c - 1 - j), 0)
    dir_f = lambda j: (0, 0, 0)
    dir_b = lambda j: (1, 0, 0)
    hbuf = pltpu.VMEM((tc * batch, 2 * S5_COLS), F32)
    state = pltpu.VMEM((batch, S5_COLS), F32)
    out = jax.ShapeDtypeStruct((batch, s, w), F32)
    return pl.pallas_call(
        functools.partial(_s5_kernel, tc=tc, batch=batch),
        out_shape=(out, out),
        grid=(nch,),
        in_specs=[pl.BlockSpec((batch, tc, w), fwd),
                  pl.BlockSpec((batch, tc, w), bwd),
                  pl.BlockSpec((1, w, 2 * S5_COLS), dir_f),
                  pl.BlockSpec((1, w, 2 * S5_COLS), dir_b),
                  pl.BlockSpec((1, 2 * S5_COLS, w), dir_f),
                  pl.BlockSpec((1, 2 * S5_COLS, w), dir_b),
                  pl.BlockSpec((1, batch, S5_COLS), dir_f),
                  pl.BlockSpec((1, batch, S5_COLS), dir_b),
                  pl.BlockSpec((1, batch, S5_COLS), dir_f),
                  pl.BlockSpec((1, batch, S5_COLS), dir_b)],
        out_specs=(pl.BlockSpec((batch, tc, w), fwd), pl.BlockSpec((batch, tc, w), bwd)),
        scratch_shapes=[hbuf, hbuf, state, state, state, state],
        compiler_params=_params("arbitrary"),
    )(us, us, bmat, bmat, cmat, cmat, a_re, a_re, a_im, a_im)


def _dft_cos_sin(n):
    k = np.arange(n, dtype=np.int64)
    ang = 2.0 * np.pi * ((k[:, None] * k[None, :]) % n).astype(np.float64) / n
    return np.cos(ang), np.sin(ang)


def _fnet_constants(l, lc):
    def seq_matrix(n):
        c, s = _dft_cos_sin(n)
        scale = 1.0 / math.sqrt(n * FNET_HEAD_DIM)
        return jnp.asarray(np.concatenate([c, -s], axis=1) * scale, dtype=F32).astype(BF16)
    c64, s64 = _dft_cos_sin(FNET_HEAD_DIM)
    eye = np.eye(FNET_HEADS)
    return (seq_matrix(l), seq_matrix(lc),
            jnp.asarray(np.kron(eye, c64), dtype=F32).astype(BF16),
            jnp.asarray(np.kron(eye, s64), dtype=F32).astype(BF16))


def _fnet_kernel(u_ref, csl_ref, csc_ref, c64_ref, s64_ref, w_ref, o_ref, tl_ref, tc_ref, *, lc, l):
    i = pl.program_id(1)
    w = w_ref[...]

    @pl.when(i == 0)
    def _():
        u = u_ref[0]
        t_cos = jnp.dot(u, c64_ref[...], preferred_element_type=F32).astype(BF16)
        t_sin = jnp.dot(u, s64_ref[...], preferred_element_type=F32).astype(BF16)
        tc_ref[0:lc, :] = t_cos[0:lc]
        tc_ref[lc:2 * lc, :] = t_sin[0:lc]
        tl_ref[0:l, :] = t_cos[lc:]
        tl_ref[l:2 * l, :] = t_sin[lc:]
        z = jnp.dot(csc_ref[...], tc_ref[...], preferred_element_type=F32)
        o_ref[0] = jnp.dot(z.astype(BF16), w, preferred_element_type=F32).astype(BF16)

    @pl.when(i > 0)
    def _():
        z = jnp.dot(csl_ref[...], tl_ref[...], preferred_element_type=F32)
        o_ref[0] = jnp.dot(z.astype(BF16), w, preferred_element_type=F32).astype(BF16)


def _fourier(uf, csl, csc, c64, s64, wbd, lc):
    b, s, w = uf.shape
    l = s - lc
    tmf = lc
    nt = 1 + l // tmf
    const = lambda bb, i: (0, 0)
    return pl.pallas_call(
        functools.partial(_fnet_kernel, lc=lc, l=l),
        out_shape=jax.ShapeDtypeStruct((b, s, w), BF16),
        grid=(b, nt),
        in_specs=[pl.BlockSpec((1, s, w), lambda bb, i: (bb, 0, 0)),
                  pl.BlockSpec((tmf, 2 * l), lambda bb, i: (jnp.maximum(i - 1, 0), 0)),
                  pl.BlockSpec((lc, 2 * lc), const),
                  pl.BlockSpec((w, w), const),
                  pl.BlockSpec((w, w), const),
                  pl.BlockSpec((w, w), const)],
        out_specs=pl.BlockSpec((1, tmf, w), lambda bb, i: (bb, i, 0)),
        scratch_shapes=[pltpu.VMEM((2 * l, w), BF16), pltpu.VMEM((2 * lc, w), BF16)],
        compiler_params=_params("arbitrary", "arbitrary"),
    )(uf, csl, csc, c64, s64, wbd)


def _gelu_tanh(x):
    return 0.5 * x * (1.0 + jnp.tanh(math.sqrt(2.0 / math.pi) * (x + 0.044715 * (x * x * x))))


def _route(logits_t, b_router_ref):
    e_all = range(N_EXPERTS)
    aff = [jax.nn.sigmoid(logits_t[e:e + 1, :]) for e in e_all]
    sel = [aff[e] + b_router_ref[e] for e in e_all]
    epg = EXPERTS_PER_GROUP
    keep = []
    for e in e_all:
        g0 = (e // epg) * epg
        rank = jnp.zeros_like(sel[e], dtype=jnp.int32)
        for o in range(g0, g0 + epg):
            if o == e:
                continue
            ahead = (sel[o] >= sel[e]) if o < e else (sel[o] > sel[e])
            rank = rank + ahead.astype(jnp.int32)
        keep.append(rank < 2)
    score = []
    for g in range(N_EXPERT_GROUPS):
        s0, s1, s2, s3 = sel[g * epg:(g + 1) * epg]
        hi01, lo01 = jnp.maximum(s0, s1), jnp.minimum(s0, s1)
        hi23, lo23 = jnp.maximum(s2, s3), jnp.minimum(s2, s3)
        top1 = jnp.maximum(hi01, hi23)
        top2 = jnp.maximum(jnp.minimum(hi01, hi23), jnp.maximum(lo01, lo23))
        score.append(top1 + top2)
    chosen = []
    for g in range(N_EXPERT_GROUPS):
        best = None
        for o in range(N_EXPERT_GROUPS):
            if o == g:
                continue
            cond = (score[g] > score[o]) if o < g else (score[g] >= score[o])
            best = cond if best is None else (best & cond)
        chosen.append(best)
    picked = [jnp.where(chosen[e // epg] & keep[e], aff[e], 0.0) for e in e_all]
    total = picked[0]
    for e in range(1, N_EXPERTS):
        total = total + picked[e]
    group = jnp.zeros_like(total)
    for g in range(1, N_EXPERT_GROUPS):
        group = jnp.where(chosen[g], float(g), group)
    return [p / total for p in picked], group


def _outproj_kernel(x_ref, att_ref, yf_ref, yb_ref, us_ref, fou_ref, mod_ref, sd_ref, wglu_ref, bglu_ref,
                    wout_ref, g2_ref, wr_ref, br_ref, ident_ref,
                    x1_ref, h2_ref, gate_ref, gate_t_ref):
    mod = mod_ref[0, 0]
    y = sd_ref[...] * us_ref[0] + yf_ref[0] + yb_ref[0]
    g = _gelu_tanh(y)
    ssm = g * jax.nn.sigmoid(jnp.dot(g.astype(BF16), wglu_ref[...], preferred_element_type=F32)
                             + bglu_ref[...])
    a0, a1 = NA_WIDTH, NA_WIDTH + S5_WIDTH
    o = (jnp.dot(att_ref[0], wout_ref[0:a0, :], preferred_element_type=F32)
         + jnp.dot(ssm.astype(BF16), wout_ref[a0:a1, :], preferred_element_type=F32)
         + jnp.dot(fou_ref[0], wout_ref[a1:, :], preferred_element_type=F32))
    x1 = x_ref[0] + mod[2:3] * o
    x1_ref[0] = x1
    r = lax.rsqrt(jnp.mean(x1 * x1, axis=-1, keepdims=True) + EPS)
    h2 = (x1 * r * g2_ref[...]) * (1.0 + mod[4:5]) + mod[3:4]
    h2_ref[0] = h2.astype(BF16)

    logits_t = _dot_f32(wr_ref[...], h2, NT_DIMS)
    gates, group = _route(logits_t, br_ref)
    tm = h2.shape[0]
    sub = lax.broadcasted_iota(jnp.int32, (LANES, tm), 0)
    gt = jnp.zeros((LANES, tm), F32)
    for e in range(N_EXPERTS):
        gt = jnp.where(sub == e, gates[e], gt)
    gt = jnp.where(sub == GROUP_ROW, group, gt)
    gate_t_ref[0, 0] = gt
    p0 = gt.astype(BF16)
    r1 = gt - p0.astype(F32)
    p1 = r1.astype(BF16)
    p2 = (r1 - p1.astype(F32)).astype(BF16)
    ident = ident_ref[...]
    nt = functools.partial(lax.dot_general, dimension_numbers=NT_DIMS, preferred_element_type=F32)
    gate_ref[0] = nt(ident, p0) + nt(ident, p1) + nt(ident, p2)


def _out_projection(xa, att, y_fwd, y_bwd, us, fou, modtab, s5_d, w_glu_bf, b_glu, w_out_bf, g2, w_router_t, b_router,
                    ident, tm, nct):
    b, s, d = xa.shape
    nt = s // tm
    row = lambda i, bb: (bb, i, 0)
    const2 = lambda i, bb: (0, 0)
    return pl.pallas_call(
        _outproj_kernel,
        out_shape=(jax.ShapeDtypeStruct((b, s, d), F32),
                   jax.ShapeDtypeStruct((b, s, d), BF16),
                   jax.ShapeDtypeStruct((b, s, LANES), F32),
                   jax.ShapeDtypeStruct((b, nt, LANES, tm), F32)),
        grid=(nt, b),
        in_specs=[pl.BlockSpec((1, tm, d), row),
                  pl.BlockSpec((1, tm, NA_WIDTH), row),
                  pl.BlockSpec((1, tm, S5_WIDTH), row),
                  pl.BlockSpec((1, tm, S5_WIDTH), row),
                  pl.BlockSpec((1, tm, S5_WIDTH), row),
                  pl.BlockSpec((1, tm, FNET_WIDTH), row),
                  pl.BlockSpec((1, 1, 6, d), lambda i, bb: (bb, jnp.where(i >= nct, 1, 0), 0, 0)),
                  pl.BlockSpec((1, S5_WIDTH), const2),
                  pl.BlockSpec((S5_WIDTH, S5_WIDTH), const2),
                  pl.BlockSpec((1, S5_WIDTH), const2),
                  pl.BlockSpec((d, d), const2),
                  pl.BlockSpec((1, d), const2),
                  pl.BlockSpec((N_EXPERTS, d), const2),
                  pl.BlockSpec(memory_space=pltpu.SMEM),
                  pl.BlockSpec((tm, tm), const2)],
        out_specs=(pl.BlockSpec((1, tm, d), row),
                   pl.BlockSpec((1, tm, d), row),
                   pl.BlockSpec((1, tm, LANES), row),
                   pl.BlockSpec((1, 1, LANES, tm), lambda i, bb: (bb, i, 0, 0))),
        compiler_params=_params("arbitrary", "arbitrary"),
    )(xa, att, y_fwd, y_bwd, us, fou, modtab, s5_d, w_glu_bf, b_glu, w_out_bf, g2, w_router_t, b_router, ident)


MOE_BLK = SUBLANES
MOE_ROWS = 512
META_LANES = 16


def _moe_local_rows(tm):
    need = tm + N_EXPERT_GROUPS * (MOE_BLK - 1)
    return -(-need // LANES) * LANES


def _moe_sorted_rows(n_tiles, tm):
    need = n_tiles * (tm + N_EXPERT_GROUPS * (MOE_BLK - 1))
    return -(-need // MOE_ROWS) * MOE_ROWS + N_EXPERT_GROUPS * MOE_ROWS


def _moe_plan(group_ids, n_sorted):
    ng = N_EXPERT_GROUPS
    onehot = group_ids[:, :, None] == jnp.arange(ng, dtype=jnp.int32)[None, None, :]
    count = jnp.sum(onehot.astype(jnp.int32), axis=1)
    padded = (count + (MOE_BLK - 1)) // MOE_BLK * MOE_BLK
    local_start = jnp.cumsum(padded, axis=1) - padded
    before = jnp.cumsum(padded, axis=0) - padded
    region = (jnp.sum(padded, axis=0) + (MOE_ROWS - 1)) // MOE_ROWS * MOE_ROWS
    region_end = jnp.cumsum(region)
    offset = (region_end - region)[None, :] + before
    meta = jnp.concatenate([local_start, offset, padded // MOE_BLK, jnp.zeros_like(count)], axis=1)
    tile_row = jnp.arange(n_sorted // MOE_ROWS, dtype=jnp.int32) * MOE_ROWS
    tile_group = jnp.minimum(jnp.sum((tile_row[:, None] >= region_end[None, :]).astype(jnp.int32), axis=1), ng - 1)
    n_used = (region_end[-1] // MOE_ROWS).reshape(1)
    return meta.astype(jnp.int32).reshape(-1, 1, META_LANES), tile_group.astype(jnp.int32), n_used.astype(jnp.int32)


def _moe_block_moves(meta_ref, make_copy, action):
    ng = N_EXPERT_GROUPS
    for g in range(ng):
        local0 = meta_ref[0, 0, g]
        sorted0 = meta_ref[0, 0, ng + g]
        nblk = meta_ref[0, 0, 2 * ng + g]

        def body(k, carry, local0=local0, sorted0=sorted0):
            local = pl.multiple_of(local0 + k * MOE_BLK, MOE_BLK)
            srt = pl.multiple_of(sorted0 + k * MOE_BLK, MOE_BLK)
            action(make_copy(local, srt))
            return carry

        lax.fori_loop(0, nblk, body, 0)


def _one_hot_bf16(mask):
    return jnp.where(mask, 1.0, 0.0).astype(BF16)


def _split3_bf16(x):
    p0 = x.astype(BF16)
    r1 = x - p0.astype(F32)
    p1 = r1.astype(BF16)
    p2 = (r1 - p1.astype(F32)).astype(BF16)
    return p0, p1, p2


def _dispatch_kernel(meta_ref, prev_ref, prev2_ref, h_ref, gate_ref, gate_t_ref, upper_ref, init_ref, xs_ref,
                     buf, sem, *, rloc):
    del init_ref
    step = pl.program_id(0) * pl.num_programs(1) + pl.program_id(1)
    last = pl.num_programs(0) * pl.num_programs(1) - 1
    slot = step % 2
    ng = N_EXPERT_GROUPS
    tm = h_ref.shape[1]

    def copy_from(slot_):
        return lambda local, srt: pltpu.make_async_copy(
            buf.at[slot_, pl.ds(local, MOE_BLK), :], xs_ref.at[pl.ds(srt, MOE_BLK), :], sem.at[slot_])

    @pl.when(step >= 2)
    def _():
        _moe_block_moves(prev2_ref, copy_from(slot), lambda c: c.wait())

    gid = gate_t_ref[0, 0][GROUP_ROW:GROUP_ROW + 1, :]
    sub = lax.broadcasted_iota(jnp.int32, (SUBLANES, tm), 0)
    mine = sub == gid.astype(jnp.int32)
    seen = jnp.dot(_one_hot_bf16(mine), upper_ref[...], preferred_element_type=F32)
    start = jnp.zeros((SUBLANES, tm), jnp.int32)
    for g in range(ng):
        start = jnp.where(sub == g, meta_ref[0, 0, g], start)
    dest = jnp.sum(jnp.where(mine, seen + start.astype(F32), 0.0), axis=0, keepdims=True)
    pick = _one_hot_bf16(lax.broadcasted_iota(jnp.int32, (rloc, tm), 0) == dest.astype(jnp.int32))

    d = h_ref.shape[2]
    buf[slot, :, 0:d] = jnp.dot(pick, h_ref[0], preferred_element_type=F32)
    p0, p1, p2 = _split3_bf16(gate_ref[0])
    dot = functools.partial(jnp.dot, preferred_element_type=F32)
    buf[slot, :, d:] = dot(pick, p0) + dot(pick, p1) + dot(pick, p2)

    _moe_block_moves(meta_ref, copy_from(slot), lambda c: c.start())

    @pl.when(step == last)
    def _():
        _moe_block_moves(meta_ref, copy_from(slot), lambda c: c.wait())

        @pl.when(step >= 1)
        def _():
            _moe_block_moves(prev_ref, copy_from(1 - slot), lambda c: c.wait())


def _dispatch(meta, h2, gates, gates_t, upper, n_sorted, i0):
    b, s, d = h2.shape
    tm = gates_t.shape[3]
    nta = s // tm - i0
    rloc = _moe_local_rows(tm)
    width = d + LANES

    def meta_at(delta):
        return lambda bb, i: (jnp.maximum(bb * nta + i - delta, 0), 0, 0)

    row = lambda bb, i: (bb, i + i0, 0)
    return pl.pallas_call(
        functools.partial(_dispatch_kernel, rloc=rloc),
        out_shape=jax.ShapeDtypeStruct((n_sorted, width), F32),
        grid=(b, nta),
        in_specs=[pl.BlockSpec((1, 1, META_LANES), meta_at(0), memory_space=pltpu.SMEM),
                  pl.BlockSpec((1, 1, META_LANES), meta_at(1), memory_space=pltpu.SMEM),
                  pl.BlockSpec((1, 1, META_LANES), meta_at(2), memory_space=pltpu.SMEM),
                  pl.BlockSpec((1, tm, d), row),
                  pl.BlockSpec((1, tm, LANES), row),
                  pl.BlockSpec((1, 1, LANES, tm), lambda bb, i: (bb, i + i0, 0, 0)),
                  pl.BlockSpec((tm, tm), lambda bb, i: (0, 0)),
                  pl.BlockSpec(memory_space=pl.ANY)],
        out_specs=pl.BlockSpec(memory_space=pl.ANY),
        scratch_shapes=[pltpu.VMEM((2, rloc, width), F32), pltpu.SemaphoreType.DMA((2,))],
        input_output_aliases={7: 0},
        compiler_params=_params("arbitrary", "arbitrary"),
    )(meta, meta, meta, h2, gates, gates_t, upper, jnp.zeros((n_sorted, width), F32))


def _group_ffn_kernel(tile_group_ref, n_used_ref, xs_ref, wg_ref, wu_ref, wd_ref, o_ref):
    i = pl.program_id(0)
    d = o_ref.shape[1]

    @pl.when(i < n_used_ref[0])
    def _():
        x = xs_ref[:, 0:d].astype(BF16)
        gates = xs_ref[:, d:]
        lane = lax.broadcasted_iota(jnp.int32, (1, LANES), 1)
        first = tile_group_ref[i] * EXPERTS_PER_GROUP
        acc = None
        for k in range(EXPERTS_PER_GROUP):
            ge = jnp.sum(jnp.where(lane == first + k, gates, 0.0), axis=-1, keepdims=True)
            gp = jnp.dot(x, wg_ref[0, k], preferred_element_type=F32)
            up = jnp.dot(x, wu_ref[0, k], preferred_element_type=F32)
            he = (gp * jax.nn.sigmoid(gp)) * up * ge
            part = jnp.dot(he.astype(BF16), wd_ref[0, k], preferred_element_type=F32)
            acc = part if acc is None else acc + part
        o_ref[...] = acc

    @pl.when(i >= n_used_ref[0])
    def _():
        o_ref[...] = jnp.zeros_like(o_ref)


def _group_ffn(tile_group, n_used, xs, wg, wu, wd):
    n_sorted, width = xs.shape
    d = width - LANES
    ff = wg.shape[-1]
    epg = EXPERTS_PER_GROUP
    per_group = lambda i, tg, nu: (tg[i], 0, 0, 0)
    rows = lambda i, tg, nu: (i, 0)
    grid_spec = pltpu.PrefetchScalarGridSpec(
        num_scalar_prefetch=2,
        grid=(n_sorted // MOE_ROWS,),
        in_specs=[pl.BlockSpec((MOE_ROWS, width), rows),
                  pl.BlockSpec((1, epg, d, ff), per_group),
                  pl.BlockSpec((1, epg, d, ff), per_group),
                  pl.BlockSpec((1, epg, ff, d), per_group)],
        out_specs=pl.BlockSpec((MOE_ROWS, d), rows))
    return pl.pallas_call(
        _group_ffn_kernel,
        out_shape=jax.ShapeDtypeStruct((n_sorted, d), F32),
        grid_spec=grid_spec,
        compiler_params=_params("arbitrary"),
    )(tile_group, n_used, xs, wg, wu, wd)


def _combine_kernel(meta_ref, next_ref, gate_ref, x1_ref, mod_ref, lower_ref, ys_ref, o_ref, buf, sem):
    step = pl.program_id(0) * pl.num_programs(1) + pl.program_id(1)
    last = pl.num_programs(0) * pl.num_programs(1) - 1
    slot = step % 2
    ng = N_EXPERT_GROUPS
    rloc = buf.shape[1]

    def copy_into(slot_):
        return lambda local, srt: pltpu.make_async_copy(
            ys_ref.at[pl.ds(srt, MOE_BLK), :], buf.at[slot_, pl.ds(local, MOE_BLK), :], sem.at[slot_])

    @pl.when(step == 0)
    def _():
        buf[...] = jnp.zeros_like(buf)
        _moe_block_moves(meta_ref, copy_into(0), lambda c: c.start())

    @pl.when(step < last)
    def _():
        _moe_block_moves(next_ref, copy_into(1 - slot), lambda c: c.start())

    gates = gate_ref[0]
    gid = gates[:, GROUP_ROW:GROUP_ROW + 1].astype(jnp.int32)
    lane = lax.broadcasted_iota(jnp.int32, (1, LANES), 1)
    mine = lane == gid
    seen = jnp.dot(lower_ref[...], _one_hot_bf16(mine), preferred_element_type=F32)
    start = jnp.zeros((1, LANES), jnp.int32)
    for g in range(ng):
        start = jnp.where(lane == g, meta_ref[0, 0, g], start)
    dest = jnp.sum(jnp.where(mine, seen + start.astype(F32), 0.0), axis=-1, keepdims=True)
    pick = _one_hot_bf16(lax.broadcasted_iota(jnp.int32, (1, rloc), 1) == dest.astype(jnp.int32))

    _moe_block_moves(meta_ref, copy_into(slot), lambda c: c.wait())
    y_hi, y_lo = _split_bf16(buf[slot])
    y = (jnp.dot(pick, y_hi, preferred_element_type=F32) + jnp.dot(pick, y_lo, preferred_element_type=F32))
    o_ref[0] = x1_ref[0] + mod_ref[0, 0, 5:6, :] * y


def _combine(meta, gates, x1, modtab, lower, ys, i0, nct):
    b, s, d = x1.shape
    tm = lower.shape[0]
    nta = s // tm - i0
    n_meta = meta.shape[0]
    rloc = _moe_local_rows(tm)

    def meta_at(delta):
        return lambda bb, i: (jnp.minimum(bb * nta + i + delta, n_meta - 1), 0, 0)

    row = lambda bb, i: (bb, i + i0, 0)
    return pl.pallas_call(
        _combine_kernel,
        out_shape=jax.ShapeDtypeStruct((b, nta * tm, d), F32),
        grid=(b, nta),
        in_specs=[pl.BlockSpec((1, 1, META_LANES), meta_at(0), memory_space=pltpu.SMEM),
                  pl.BlockSpec((1, 1, META_LANES), meta_at(1), memory_space=pltpu.SMEM),
                  pl.BlockSpec((1, tm, LANES), row),
                  pl.BlockSpec((1, tm, d), row),
                  pl.BlockSpec((1, 1, 6, d), lambda bb, i: (bb, jnp.where(i + i0 >= nct, 1, 0), 0, 0)),
                  pl.BlockSpec((tm, tm), lambda bb, i: (0, 0)),
                  pl.BlockSpec(memory_space=pl.ANY)],
        out_specs=pl.BlockSpec((1, tm, d), lambda bb, i: (bb, i, 0)),
        scratch_shapes=[pltpu.VMEM((2, rloc, d), F32), pltpu.SemaphoreType.DMA((2,))],
        compiler_params=_params("arbitrary", "arbitrary"),
    )(meta, meta, gates, x1, modtab, lower, ys)


def _experts(h2, gates, gates_t, x1, modtab, wg, wu, wd, upper, lower, i0, nct):
    b, s, d = x1.shape
    tm = gates_t.shape[3]
    nta = s // tm - i0
    n_sorted = _moe_sorted_rows(b * nta, tm)
    group_ids = gates[:, i0 * tm:, GROUP_ROW].astype(jnp.int32).reshape(b * nta, tm)
    meta, tile_group, n_used = _moe_plan(group_ids, n_sorted)
    xs = _dispatch(meta, h2, gates, gates_t, upper, n_sorted, i0)
    ys = _group_ffn(tile_group, n_used, xs, wg, wu, wd)
    return _combine(meta, gates, x1, modtab, lower, ys, i0, nct)


def _rope_tables(l, lc):
    t = jnp.arange(l, dtype=jnp.int32)
    row = (t // GRID_W).astype(F32)
    col = (t % GRID_W).astype(F32)
    inv_freq = ROPE_BASE ** (-jnp.arange(ROPE_PAIRS_PER_AXIS, dtype=F32) / ROPE_PAIRS_PER_AXIS)
    ang = jnp.concatenate([row[:, None] * inv_freq, col[:, None] * inv_freq], axis=-1)
    cos, sin = jnp.cos(ang), jnp.sin(ang)
    zero = jnp.zeros_like(sin)
    cos_h = jnp.concatenate([cos, cos], axis=-1)
    sin_lo = jnp.concatenate([-sin, zero], axis=-1)
    sin_hi = jnp.concatenate([zero, sin], axis=-1)
    def full(tab, ctx_value):
        tab = jnp.tile(tab, (1, NA_HEADS))
        return jnp.concatenate([jnp.full((lc, NA_WIDTH), ctx_value, F32), tab], axis=0)
    return full(cos_h, 1.0), full(sin_lo, 0.0), full(sin_hi, 0.0)


def kernel(x, c, ctx, c_ctx, w_ada, b_ada, norm1_g, norm2_g, w_in, q_norm_g, k_norm_g, rpb, s5_lam_re, s5_lam_im, s5_log_step, s5_b_re, s5_b_im, s5_c_re, s5_c_im, s5_d, w_glu, b_glu, w_fnet, w_out, w_router, b_router, w_gate, w_up, w_down):
    b, l, d = x.shape
    lc = ctx.shape[1]
    s = lc + l
    depth = w_ada.shape[0]
    rows = l // GRID_W
    tm = min(256, lc)
    assert lc % tm == 0 and l % tm == 0 and l % GRID_W == 0
    nct = lc // tm
    rb = 4 if rows % 4 == 0 else 1
    tc = min(64, lc)

    xa = jnp.concatenate([ctx, x], axis=1)

    n_mod = -(-(b + 1) // SUBLANES) * SUBLANES
    cvec = jnp.concatenate([c, c_ctx[None, :], jnp.zeros((n_mod - b - 1, d), F32)], axis=0)
    mod_all = _modulation(cvec, w_ada, b_ada)

    cos, sin_lo, sin_hi = _rope_tables(l, lc)
    head_ones = jnp.asarray(np.kron(np.eye(NA_HEADS), np.ones((HEAD_DIM, HEAD_DIM))), dtype=BF16)
    csl, csc, c64, s64 = _fnet_constants(l, lc)
    ident = jnp.eye(tm, dtype=BF16)
    w_router_t = w_router.T
    earlier = np.triu(np.ones((tm, tm)), 1)
    upper = jnp.asarray(earlier, dtype=BF16)
    lower = jnp.asarray(earlier.T, dtype=BF16)
    by_group = lambda w: w.astype(BF16).reshape((N_EXPERT_GROUPS, EXPERTS_PER_GROUP) + w.shape[1:])

    for layer in range(depth):
        mod = mod_all[layer]
        modtab = jnp.stack([jnp.broadcast_to(mod[b].reshape(1, 6, d), (b, 6, d)),
                            mod[:b].reshape(b, 6, d)], axis=1)
        qg = jnp.tile(q_norm_g[layer] * ATTN_SCALE, NA_HEADS).reshape(1, NA_WIDTH)
        kg = jnp.tile(k_norm_g[layer], NA_HEADS).reshape(1, NA_WIDTH)
        q, k, v, us, uf = _in_projection(xa, modtab, norm1_g[layer].reshape(1, d), w_in[layer].astype(BF16),
                                         qg, kg, head_ones, cos, sin_lo, sin_hi, tm, nct)

        bias = _attn_bias_tables(rpb[layer], rows, rb)
        att = _attention(q, k, v, bias, lc, rows, rb)

        bmat, cmat, a_re, a_im = _s5_matrices(s5_lam_re[layer], s5_lam_im[layer], s5_log_step[layer],
                                              s5_b_re[layer], s5_b_im[layer], s5_c_re[layer], s5_c_im[layer], b)
        y_fwd, y_bwd = _s5_scan(us, bmat, cmat, a_re, a_im, lc, tc)

        wbd = jnp.einsum('hde,hg->hdge', w_fnet[layer], jnp.eye(FNET_HEADS, dtype=F32))
        fou = _fourier(uf, csl, csc, c64, s64, wbd.reshape(FNET_WIDTH, FNET_WIDTH).astype(BF16), lc)

        x1, h2, gates, gates_t = _out_projection(
            xa, att, y_fwd, y_bwd, us, fou, modtab, s5_d[layer].reshape(1, S5_WIDTH),
            w_glu[layer].astype(BF16), b_glu[layer].reshape(1, S5_WIDTH), w_out[layer].astype(BF16),
            norm2_g[layer].reshape(1, d), w_router_t, b_router, ident, tm, nct)

        i0 = nct if layer == depth - 1 else 0
        xa = _experts(h2, gates, gates_t, x1, modtab, by_group(w_gate[layer]), by_group(w_up[layer]),
                      by_group(w_down[layer]), upper, lower, i0, nct)

    return xa
```

```python
import functools
import math

import numpy as np
import jax
import jax.numpy as jnp
from jax import lax
from jax.experimental import pallas as pl
from jax.experimental.pallas import tpu as pltpu

F32 = jnp.float32
BF16 = jnp.bfloat16

GRID_W = 64
EPS = 1e-6
HEAD_DIM = 64
NA_WIDTH = 512
NA_HEADS = 8
NA_WIN_H = 8
NA_WIN_W = 16
S5_WIDTH = 256
S5_GROUP = 16
S5_GROUPS = 16
S5_STATE = 64
S5_COLS = S5_GROUPS * S5_STATE
FNET_WIDTH = 256
FNET_HEADS = 4
FNET_HEAD_DIM = 64
IN_COLS = 3 * NA_WIDTH + S5_WIDTH + FNET_WIDTH
ATTN_SCALE = HEAD_DIM ** -0.5
ROPE_BASE = 100.0
ROPE_PAIRS_PER_AXIS = HEAD_DIM // 4
N_EXPERTS = 16
GROUP_ROW = N_EXPERTS
N_EXPERT_GROUPS = 4
EXPERTS_PER_GROUP = 4
EXPERT_FF = 512

LANES = 128
SUBLANES = 8
VMEM_LIMIT = 56 * 1024 * 1024

NEG_BIG = -1e30
NT_DIMS = (((1,), (1,)), ((), ()))


def _split_bf16(x):
    hi = x.astype(BF16)
    lo = (x - hi.astype(F32)).astype(BF16)
    return hi, lo


def _split3_bf16(x):
    p0 = x.astype(BF16)
    r1 = x - p0.astype(F32)
    p1 = r1.astype(BF16)
    p2 = (r1 - p1.astype(F32)).astype(BF16)
    return p0, p1, p2


def _dot_f32(a, b, dims=None):
    a_hi, a_lo = _split_bf16(a)
    b_hi, b_lo = _split_bf16(b)
    if dims is None:
        dot = functools.partial(jnp.dot, preferred_element_type=F32)
    else:
        dot = functools.partial(lax.dot_general, dimension_numbers=dims, preferred_element_type=F32)
    return dot(a_hi, b_hi) + dot(a_hi, b_lo) + dot(a_lo, b_hi)


def _params(*sem):
    return pltpu.CompilerParams(dimension_semantics=sem, vmem_limit_bytes=VMEM_LIMIT)


def _mod_kernel(c_ref, w_ref, b_ref, o_ref):
    c = c_ref[...]
    s = c * jax.nn.sigmoid(c)
    o_ref[0] = _dot_f32(s, w_ref[0]) + b_ref[0]


def _modulation(cvec, w_ada, b_ada):
    depth, d, n = w_ada.shape
    r = cvec.shape[0]
    tn = 768
    return pl.pallas_call(
        _mod_kernel,
        out_shape=jax.ShapeDtypeStruct((depth, r, n), F32),
        grid=(depth, n // tn),
        in_specs=[pl.BlockSpec((r, d), lambda l, j: (0, 0)),
                  pl.BlockSpec((1, d, tn), lambda l, j: (l, 0, j)),
                  pl.BlockSpec((1, 1, tn), lambda l, j: (l, 0, j))],
        out_specs=pl.BlockSpec((1, r, tn), lambda l, j: (l, 0, j)),
        compiler_params=_params("arbitrary", "arbitrary"),
    )(cvec, w_ada, b_ada.reshape(depth, 1, n))


def _inproj_kernel(x_ref, mod_ref, g1_ref, w_ref, qg_ref, kg_ref, gm_ref, cos_ref, sa_ref, sb_ref,
                   q_ref, k_ref, v_ref, us_ref, uf_ref):
    cos = cos_ref[...]
    sin_lo = sa_ref[...]
    sin_hi = sb_ref[...]
    gm = gm_ref[...]
    nw = NA_WIDTH

    def head_norm_rope(t, g):
        ss = jnp.dot((t * t).astype(BF16), gm, preferred_element_type=F32)
        tn = t * lax.rsqrt(ss * (1.0 / HEAD_DIM) + EPS) * g
        half = HEAD_DIM // 2
        return (tn * cos + pltpu.roll(tn, nw - half, 1) * sin_lo + pltpu.roll(tn, half, 1) * sin_hi)

    for n in range(x_ref.shape[0]):
        x = x_ref[n]
        mod = mod_ref[n, 0]
        r = lax.rsqrt(jnp.mean(x * x, axis=-1, keepdims=True) + EPS)
        h = (x * r * g1_ref[...]) * (1.0 + mod[1:2]) + mod[0:1]
        proj = jnp.dot(h.astype(BF16), w_ref[...], preferred_element_type=F32)
        q_ref[n] = head_norm_rope(proj[:, 0:nw], qg_ref[...]).astype(BF16)
        k_ref[n] = head_norm_rope(proj[:, nw:2 * nw], kg_ref[...]).astype(BF16)
        v_ref[n] = proj[:, 2 * nw:3 * nw].astype(BF16)
        us_ref[n] = proj[:, 3 * nw:3 * nw + S5_WIDTH]
        uf_ref[n] = proj[:, 3 * nw + S5_WIDTH:].astype(BF16)


def _in_projection(xa, modtab, g1, w_in_bf, qg, kg, gm, cos, sin_lo, sin_hi, tm, nct, nb):
    b, s, d = xa.shape
    nt = s // tm
    nw = NA_WIDTH
    row = lambda i, bb: (bb, i, 0)
    const2 = lambda i, bb: (0, 0)
    tab = lambda i, bb: (i, 0)
    return pl.pallas_call(
        _inproj_kernel,
        out_shape=(jax.ShapeDtypeStruct((b, s, nw), BF16),
                   jax.ShapeDtypeStruct((b, s, nw), BF16),
                   jax.ShapeDtypeStruct((b, s, nw), BF16),
                   jax.ShapeDtypeStruct((b, s, S5_WIDTH), F32),
                   jax.ShapeDtypeStruct((b, s, FNET_WIDTH), BF16)),
        grid=(nt, b // nb),
        in_specs=[pl.BlockSpec((nb, tm, d), row),
                  pl.BlockSpec((nb, 1, 6, d), lambda i, bb: (bb, jnp.where(i >= nct, 1, 0), 0, 0)),
                  pl.BlockSpec((1, d), const2),
                  pl.BlockSpec((d, IN_COLS), const2),
                  pl.BlockSpec((1, nw), const2),
                  pl.BlockSpec((1, nw), const2),
                  pl.BlockSpec((nw, nw), const2),
                  pl.BlockSpec((tm, nw), tab),
                  pl.BlockSpec((tm, nw), tab),
                  pl.BlockSpec((tm, nw), tab)],
        out_specs=(pl.BlockSpec((nb, tm, nw), row),
                   pl.BlockSpec((nb, tm, nw), row),
                   pl.BlockSpec((nb, tm, nw), row),
                   pl.BlockSpec((nb, tm, S5_WIDTH), row),
                   pl.BlockSpec((nb, tm, FNET_WIDTH), row)),
        compiler_params=_params("arbitrary", "arbitrary"),
    )(xa, modtab, g1, w_in_bf, qg, kg, gm, cos, sin_lo, sin_hi)


def _attn_blocks(rows, rb):
    kh = min(NA_WIN_H, rows)
    kr = min(rb + kh - 1, rows)
    blocks, classes = [], {}
    for r0 in range(0, rows, rb):
        ks = min(max(r0 - kh // 2, 0), rows - kr)
        starts = tuple(min(max(r - kh // 2, 0), rows - kh) - ks for r in range(r0, r0 + rb))
        key = (r0 - ks, starts)
        cls = classes.setdefault(key, len(classes))
        blocks.append((r0, ks, cls))
    return kh, kr, blocks, list(classes.keys())


def _attn_bias_tables(rpb, rows, rb):
    kh, kr, blocks, class_keys = _attn_blocks(rows, rb)
    w = GRID_W
    col = np.arange(w)
    col_start = np.clip(col - NA_WIN_W // 2, 0, w - NA_WIN_W)
    n_row, n_col = 2 * NA_WIN_H - 1, 2 * NA_WIN_W - 1
    col_off = np.clip(col[None, :] - col[:, None] + (NA_WIN_W - 1), 0, n_col - 1)
    col_pick = jnp.asarray(np.eye(n_col)[col_off].reshape(w * w, n_col), dtype=F32)
    col_valid = (col[None, :] >= col_start[:, None]) & (col[None, :] < col_start[:, None] + NA_WIN_W)
    tabs = []
    for q_minus_ks, starts in class_keys:
        rq = np.arange(rb)[:, None]
        i = np.arange(kr)[None, :]
        rs = np.asarray(starts)[:, None]
        row_valid = (i >= rs) & (i < rs + kh)
        row_off = np.clip(i - (q_minus_ks + rq) + (NA_WIN_H - 1), 0, n_row - 1)
        row_pick = jnp.asarray(np.eye(n_row)[row_off].reshape(rb * kr, n_row), dtype=F32)
        bias = jnp.einsum('hyx,ny,mx->hnm', rpb.astype(F32), row_pick, col_pick,
                          precision=lax.Precision.HIGHEST)
        bias = bias.reshape(-1, rb, kr, w, w).transpose(0, 1, 3, 2, 4)
        valid = row_valid[:, None, :, None] & col_valid[None, :, None, :]
        tabs.append(jnp.where(valid[None], bias, NEG_BIG).reshape(-1, rb * w, kr * w))
    return jnp.stack(tabs, axis=0)


def _attn_kernel(q_ref, k_ref, v_ref, bias_ref, o_ref, *, lc, rb, kr, blocks):
    lane = lax.broadcasted_iota(jnp.int32, (1, LANES), 1)
    first_head = lane < HEAD_DIM

    def stack_heads(qb):
        z = jnp.zeros_like(qb)
        return jnp.concatenate([jnp.where(first_head, qb, z), jnp.where(first_head, z, qb)], axis=0)

    def unstack_heads(o, n):
        return jnp.where(first_head, o[:n], o[n:])

    kc = k_ref[0, 0:lc, :]
    vc = v_ref[0, 0:lc, :]

    s = lax.dot_general(stack_heads(q_ref[0, 0:lc, :]), kc, NT_DIMS, preferred_element_type=F32)
    m = jnp.max(s, axis=-1, keepdims=True)
    p = jnp.exp(s - m)
    den = jnp.sum(p, axis=-1, keepdims=True)
    o = jnp.dot(p.astype(BF16), vc, preferred_element_type=F32) / den
    o_ref[0, 0:lc, :] = unstack_heads(o, lc).astype(BF16)

    nq = rb * GRID_W
    nk = kr * GRID_W
    for r0, ks, cls in blocks:
        q0 = lc + r0 * GRID_W
        k0 = lc + ks * GRID_W
        q2 = stack_heads(q_ref[0, q0:q0 + nq, :])
        k_loc = k_ref[0, k0:k0 + nk, :]
        v_loc = v_ref[0, k0:k0 + nk, :]
        s_loc = lax.dot_general(q2, k_loc, NT_DIMS, preferred_element_type=F32)
        s_loc = s_loc + bias_ref[cls].reshape(2 * nq, nk)
        s_ctx = lax.dot_general(q2, kc, NT_DIMS, preferred_element_type=F32)
        m = jnp.maximum(jnp.max(s_loc, axis=-1, keepdims=True), jnp.max(s_ctx, axis=-1, keepdims=True))
        p_loc = jnp.exp(s_loc - m)
        p_ctx = jnp.exp(s_ctx - m)
        den = jnp.sum(p_loc, axis=-1, keepdims=True) + jnp.sum(p_ctx, axis=-1, keepdims=True)
        o = (jnp.dot(p_loc.astype(BF16), v_loc, preferred_element_type=F32)
             + jnp.dot(p_ctx.astype(BF16), vc, preferred_element_type=F32)) / den
        o_ref[0, q0:q0 + nq, :] = unstack_heads(o, nq).astype(BF16)


def _attention(q, k, v, bias, lc, rows, rb):
    b, s, nw = q.shape
    _, kr, blocks, _ = _attn_blocks(rows, rb)
    ncls = bias.shape[0]
    spec = pl.BlockSpec((1, s, LANES), lambda bb, p: (bb, 0, p))
    return pl.pallas_call(
        functools.partial(_attn_kernel, lc=lc, rb=rb, kr=kr, blocks=blocks),
        out_shape=jax.ShapeDtypeStruct((b, s, nw), BF16),
        grid=(b, nw // LANES),
        in_specs=[spec, spec, spec,
                  pl.BlockSpec((ncls, 2, rb * GRID_W, kr * GRID_W), lambda bb, p: (0, p, 0, 0))],
        out_specs=spec,
        compiler_params=_params("arbitrary", "arbitrary"),
    )(q, k, v, bias)


def _s5_matrices(lam_re, lam_im, log_step, b_re, b_im, c_re, c_im, batch):
    lam_re = lam_re.astype(F32)
    lam_im = lam_im.astype(F32)
    step = jnp.exp(log_step.astype(F32))[..., None]
    mag = jnp.exp(lam_re * step)
    a_re = mag * jnp.cos(lam_im * step)
    a_im = mag * jnp.sin(lam_im * step)
    den = lam_re * lam_re + lam_im * lam_im
    co_re = ((a_re - 1.0) * lam_re + a_im * lam_im) / den
    co_im = (a_im * lam_re - (a_re - 1.0) * lam_im) / den
    b_re = b_re.astype(F32)
    b_im = b_im.astype(F32)
    bb_re = co_re[..., None] * b_re - co_im[..., None] * b_im
    bb_im = co_re[..., None] * b_im + co_im[..., None] * b_re
    eye = jnp.eye(S5_GROUPS, dtype=F32)
    to_in = lambda t: jnp.einsum('dgpc,gh->dgchp', t, eye).reshape(2, S5_WIDTH, S5_COLS)
    bmat = jnp.concatenate([to_in(bb_re), to_in(bb_im)], axis=-1)
    to_out = lambda t: jnp.einsum('dgcp,gh->dgphc', t.astype(F32), eye).reshape(2, S5_COLS, S5_WIDTH)
    cmat = jnp.concatenate([to_out(c_re), -to_out(c_im)], axis=1)
    bc = lambda t: jnp.broadcast_to(t.reshape(2, 1, S5_COLS), (2, batch, S5_COLS))
    return bmat.astype(BF16), cmat.astype(BF16), bc(a_re), bc(a_im)


def _s5_direction(u_ref, bm_ref, cm_ref, ar_ref, ai_ref, y_ref, hbuf, st_re, st_im, tc, batch, reverse):
    u = jnp.swapaxes(u_ref[...], 0, 1).reshape(tc * batch, S5_WIDTH)
    hbuf[...] = jnp.dot(u.astype(BF16), bm_ref[0], preferred_element_type=F32)

    ncol = 2 * LANES
    for cg in range(S5_COLS // ncol):
        re = slice(cg * ncol, (cg + 1) * ncol)
        im = slice(S5_COLS + cg * ncol, S5_COLS + (cg + 1) * ncol)
        h_re = st_re[:, re]
        h_im = st_im[:, re]
        for tt in range(tc):
            t = tc - 1 - tt if reverse else tt
            rows = slice(t * batch, (t + 1) * batch)
            a_re = ar_ref[0, :, re]
            a_im = ai_ref[0, :, re]
            n_re = a_re * h_re - a_im * h_im + hbuf[rows, re]
            n_im = a_re * h_im + a_im * h_re + hbuf[rows, im]
            hbuf[rows, re] = n_re
            hbuf[rows, im] = n_im
            h_re, h_im = n_re, n_im
        st_re[:, re] = h_re
        st_im[:, re] = h_im

    y = jnp.dot(hbuf[...].astype(BF16), cm_ref[0], preferred_element_type=F32)
    y_ref[...] = jnp.swapaxes(y.reshape(tc, batch, S5_WIDTH), 0, 1)


def _s5_kernel(uf_ref, ub_ref, bmf_ref, bmb_ref, cmf_ref, cmb_ref, arf_ref, arb_ref, aif_ref, aib_ref,
               yf_ref, yb_ref, hbuf_f, hbuf_b, sf_re, sf_im, sb_re, sb_im, *, tc, batch):
    @pl.when(pl.program_id(0) == 0)
    def _():
        for st in (sf_re, sf_im, sb_re, sb_im):
            st[...] = jnp.zeros_like(st)

    _s5_direction(uf_ref, bmf_ref, cmf_ref, arf_ref, aif_ref, yf_ref, hbuf_f, sf_re, sf_im, tc, batch, False)
    _s5_direction(ub_ref, bmb_ref, cmb_ref, arb_ref, aib_ref, yb_ref, hbuf_b, sb_re, sb_im, tc, batch, True)


def _s5_scan(us, bmat, cmat, a_re, a_im, lc, tc):
    batch, s, w = us.shape
    nch = s // tc
    nch_c = lc // tc
    fwd = lambda j: (0, j, 0)
    bwd = lambda j: (0, jnp.where(j < nch_c, nch_c - 1 - j, nch + nch_c - 1 - j), 0)
    dir_f = lambda j: (0, 0, 0)
    dir_b = lambda j: (1, 0, 0)
    hbuf = pltpu.VMEM((tc * batch, 2 * S5_COLS), F32)
    state = pltpu.VMEM((batch, S5_COLS), F32)
    out = jax.ShapeDtypeStruct((batch, s, w), F32)
    return pl.pallas_call(
        functools.partial(_s5_kernel, tc=tc, batch=batch),
        out_shape=(out, out),
        grid=(nch,),
        in_specs=[pl.BlockSpec((batch, tc, w), fwd),
                  pl.BlockSpec((batch, tc, w), bwd),
                  pl.BlockSpec((1, w, 2 * S5_COLS), dir_f),
                  pl.BlockSpec((1, w, 2 * S5_COLS), dir_b),
                  pl.BlockSpec((1, 2 * S5_COLS, w), dir_f),
                  pl.BlockSpec((1, 2 * S5_COLS, w), dir_b),
                  pl.BlockSpec((1, batch, S5_COLS), dir_f),
                  pl.BlockSpec((1, batch, S5_COLS), dir_b),
                  pl.BlockSpec((1, batch, S5_COLS), dir_f),
                  pl.BlockSpec((1, batch, S5_COLS), dir_b)],
        out_specs=(pl.BlockSpec((batch, tc, w), fwd), pl.BlockSpec((batch, tc, w), bwd)),
        scratch_shapes=[hbuf, hbuf, state, state, state, state],
        compiler_params=_params("arbitrary"),
    )(us, us, bmat, bmat, cmat, cmat, a_re, a_re, a_im, a_im)


def _dft_cos_sin(n):
    k = np.arange(n, dtype=np.int64)
    ang = 2.0 * np.pi * ((k[:, None] * k[None, :]) % n).astype(np.float64) / n
    return np.cos(ang), np.sin(ang)


def _fnet_constants(l, lc):
    def seq_matrix(n):
        c, s = _dft_cos_sin(n)
        scale = 1.0 / math.sqrt(n * FNET_HEAD_DIM)
        return jnp.asarray(np.concatenate([c, -s], axis=1) * scale, dtype=F32).astype(BF16)
    c64, s64 = _dft_cos_sin(FNET_HEAD_DIM)
    eye = np.eye(FNET_HEADS)
    return (seq_matrix(l), seq_matrix(lc),
            jnp.asarray(np.kron(eye, c64), dtype=F32).astype(BF16),
            jnp.asarray(np.kron(eye, s64), dtype=F32).astype(BF16))


def _fnet_kernel(u_ref, csl_ref, csc_ref, c64_ref, s64_ref, w_ref, o_ref, tl_ref, tc_ref, *, lc, l):
    i = pl.program_id(1)
    w = w_ref[...]

    @pl.when(i == 0)
    def _():
        u = u_ref[0]
        t_cos = jnp.dot(u, c64_ref[...], preferred_element_type=F32).astype(BF16)
        t_sin = jnp.dot(u, s64_ref[...], preferred_element_type=F32).astype(BF16)
        tc_ref[0:lc, :] = t_cos[0:lc]
        tc_ref[lc:2 * lc, :] = t_sin[0:lc]
        tl_ref[0:l, :] = t_cos[lc:]
        tl_ref[l:2 * l, :] = t_sin[lc:]
        z = jnp.dot(csc_ref[...], tc_ref[...], preferred_element_type=F32)
        o_ref[0] = jnp.dot(z.astype(BF16), w, preferred_element_type=F32).astype(BF16)

    @pl.when(i > 0)
    def _():
        z = jnp.dot(csl_ref[...], tl_ref[...], preferred_element_type=F32)
        o_ref[0] = jnp.dot(z.astype(BF16), w, preferred_element_type=F32).astype(BF16)


def _fourier(uf, csl, csc, c64, s64, wbd, lc):
    b, s, w = uf.shape
    l = s - lc
    tmf = lc
    nt = 1 + l // tmf
    const = lambda bb, i: (0, 0)
    return pl.pallas_call(
        functools.partial(_fnet_kernel, lc=lc, l=l),
        out_shape=jax.ShapeDtypeStruct((b, s, w), BF16),
        grid=(b, nt),
        in_specs=[pl.BlockSpec((1, s, w), lambda bb, i: (bb, 0, 0)),
                  pl.BlockSpec((tmf, 2 * l), lambda bb, i: (jnp.maximum(i - 1, 0), 0)),
                  pl.BlockSpec((lc, 2 * lc), const),
                  pl.BlockSpec((w, w), const),
                  pl.BlockSpec((w, w), const),
                  pl.BlockSpec((w, w), const)],
        out_specs=pl.BlockSpec((1, tmf, w), lambda bb, i: (bb, i, 0)),
        scratch_shapes=[pltpu.VMEM((2 * l, w), BF16), pltpu.VMEM((2 * lc, w), BF16)],
        compiler_params=_params("arbitrary", "arbitrary"),
    )(uf, csl, csc, c64, s64, wbd)


def _gelu_tanh(x):
    return 0.5 * x * (1.0 + jnp.tanh(math.sqrt(2.0 / math.pi) * (x + 0.044715 * (x * x * x))))


def _route(logits_t, b_router_ref):
    e_all = range(N_EXPERTS)
    aff = [jax.nn.sigmoid(logits_t[e:e + 1, :]) for e in e_all]
    sel = [aff[e] + b_router_ref[e] for e in e_all]
    epg = EXPERTS_PER_GROUP
    keep = []
    for e in e_all:
        g0 = (e // epg) * epg
        rank = jnp.zeros_like(sel[e], dtype=jnp.int32)
        for o in range(g0, g0 + epg):
            if o == e:
                continue
            ahead = (sel[o] >= sel[e]) if o < e else (sel[o] > sel[e])
            rank = rank + ahead.astype(jnp.int32)
        keep.append(rank < 2)
    score = []
    for g in range(N_EXPERT_GROUPS):
        s0, s1, s2, s3 = sel[g * epg:(g + 1) * epg]
        hi01, lo01 = jnp.maximum(s0, s1), jnp.minimum(s0, s1)
        hi23, lo23 = jnp.maximum(s2, s3), jnp.minimum(s2, s3)
        top1 = jnp.maximum(hi01, hi23)
        top2 = jnp.maximum(jnp.minimum(hi01, hi23), jnp.maximum(lo01, lo23))
        score.append(top1 + top2)
    chosen = []
    for g in range(N_EXPERT_GROUPS):
        best = None
        for o in range(N_EXPERT_GROUPS):
            if o == g:
                continue
            cond = (score[g] > score[o]) if o < g else (score[g] >= score[o])
            best = cond if best is None else (best & cond)
        chosen.append(best)
    picked = [jnp.where(chosen[e // epg] & keep[e], aff[e], 0.0) for e in e_all]
    total = picked[0]
    for e in range(1, N_EXPERTS):
        total = total + picked[e]
    group = jnp.zeros_like(total)
    for g in range(1, N_EXPERT_GROUPS):
        group = jnp.where(chosen[g], float(g), group)
    return [p / total for p in picked], group


def _outproj_kernel(x_ref, att_ref, yf_ref, yb_ref, us_ref, fou_ref, mod_ref, sd_ref, wglu_ref, bglu_ref,
                    wout_ref, g2_ref, wr_ref, br_ref, ident_ref,
                    x1_ref, h2_ref, gate_ref, gate_t_ref):
    nb, tm = x_ref.shape[0], x_ref.shape[1]
    h2_all = []
    for n in range(nb):
        mod = mod_ref[n, 0]
        y = sd_ref[...] * us_ref[n] + yf_ref[n] + yb_ref[n]
        g = _gelu_tanh(y)
        ssm = g * jax.nn.sigmoid(jnp.dot(g.astype(BF16), wglu_ref[...], preferred_element_type=F32)
                                 + bglu_ref[...])
        a0, a1 = NA_WIDTH, NA_WIDTH + S5_WIDTH
        o = (jnp.dot(att_ref[n], wout_ref[0:a0, :], preferred_element_type=F32)
             + jnp.dot(ssm.astype(BF16), wout_ref[a0:a1, :], preferred_element_type=F32)
             + jnp.dot(fou_ref[n], wout_ref[a1:, :], preferred_element_type=F32))
        x1 = x_ref[n] + mod[2:3] * o
        x1_ref[n] = x1
        r = lax.rsqrt(jnp.mean(x1 * x1, axis=-1, keepdims=True) + EPS)
        h2 = (x1 * r * g2_ref[...]) * (1.0 + mod[4:5]) + mod[3:4]
        h2_ref[n] = h2.astype(BF16)
        h2_all.append(h2)

    h2 = jnp.concatenate(h2_all, axis=0) if nb > 1 else h2_all[0]
    nt = nb * tm
    logits_t = _dot_f32(wr_ref[...], h2, NT_DIMS)
    gates, group = _route(logits_t, br_ref)
    sub = lax.broadcasted_iota(jnp.int32, (LANES, nt), 0)
    gt = jnp.zeros((LANES, nt), F32)
    for e in range(N_EXPERTS):
        gt = jnp.where(sub == e, gates[e], gt)
    gt = jnp.where(sub == GROUP_ROW, group, gt)
    p0, p1, p2 = _split3_bf16(gt)
    ident = ident_ref[...]
    dot_nt = functools.partial(lax.dot_general, dimension_numbers=NT_DIMS, preferred_element_type=F32)
    gate = dot_nt(ident, p0) + dot_nt(ident, p1) + dot_nt(ident, p2)
    for n in range(nb):
        gate_t_ref[n, 0] = gt[:, n * tm:(n + 1) * tm]
        gate_ref[n] = gate[n * tm:(n + 1) * tm, :]


def _out_projection(xa, att, y_fwd, y_bwd, us, fou, modtab, s5_d, w_glu_bf, b_glu, w_out_bf, g2, w_router_t, b_router,
                    ident, tm, nct, nb):
    b, s, d = xa.shape
    nt = s // tm
    row = lambda i, bb: (bb, i, 0)
    const2 = lambda i, bb: (0, 0)
    return pl.pallas_call(
        _outproj_kernel,
        out_shape=(jax.ShapeDtypeStruct((b, s, d), F32),
                   jax.ShapeDtypeStruct((b, s, d), BF16),
                   jax.ShapeDtypeStruct((b, s, LANES), F32),
                   jax.ShapeDtypeStruct((b, nt, LANES, tm), F32)),
        grid=(nt, b // nb),
        in_specs=[pl.BlockSpec((nb, tm, d), row),
                  pl.BlockSpec((nb, tm, NA_WIDTH), row),
                  pl.BlockSpec((nb, tm, S5_WIDTH), row),
                  pl.BlockSpec((nb, tm, S5_WIDTH), row),
                  pl.BlockSpec((nb, tm, S5_WIDTH), row),
                  pl.BlockSpec((nb, tm, FNET_WIDTH), row),
                  pl.BlockSpec((nb, 1, 6, d), lambda i, bb: (bb, jnp.where(i >= nct, 1, 0), 0, 0)),
                  pl.BlockSpec((1, S5_WIDTH), const2),
                  pl.BlockSpec((S5_WIDTH, S5_WIDTH), const2),
                  pl.BlockSpec((1, S5_WIDTH), const2),
                  pl.BlockSpec((d, d), const2),
                  pl.BlockSpec((1, d), const2),
                  pl.BlockSpec((N_EXPERTS, d), const2),
                  pl.BlockSpec(memory_space=pltpu.SMEM),
                  pl.BlockSpec((nb * tm, nb * tm), const2)],
        out_specs=(pl.BlockSpec((nb, tm, d), row),
                   pl.BlockSpec((nb, tm, d), row),
                   pl.BlockSpec((nb, tm, LANES), row),
                   pl.BlockSpec((nb, 1, LANES, tm), lambda i, bb: (bb, i, 0, 0))),
        compiler_params=_params("arbitrary", "arbitrary"),
    )(xa, att, y_fwd, y_bwd, us, fou, modtab, s5_d, w_glu_bf, b_glu, w_out_bf, g2, w_router_t, b_router, ident)


MOE_BLK = SUBLANES
MOE_ROWS = 512
META_LANES = 16


def _moe_local_rows(tm):
    need = tm + N_EXPERT_GROUPS * (MOE_BLK - 1)
    return -(-need // LANES) * LANES


def _moe_sorted_rows(n_tiles, tm):
    need = n_tiles * (tm + N_EXPERT_GROUPS * (MOE_BLK - 1))
    return -(-need // MOE_ROWS) * MOE_ROWS + N_EXPERT_GROUPS * MOE_ROWS


def _moe_plan(group_ids, n_sorted):
    ng = N_EXPERT_GROUPS
    onehot = group_ids[:, :, None] == jnp.arange(ng, dtype=jnp.int32)[None, None, :]
    count = jnp.sum(onehot.astype(jnp.int32), axis=1)
    padded = (count + (MOE_BLK - 1)) // MOE_BLK * MOE_BLK
    local_start = jnp.cumsum(padded, axis=1) - padded
    before = jnp.cumsum(padded, axis=0) - padded
    region = (jnp.sum(padded, axis=0) + (MOE_ROWS - 1)) // MOE_ROWS * MOE_ROWS
    region_end = jnp.cumsum(region)
    offset = (region_end - region)[None, :] + before
    meta = jnp.concatenate([local_start, offset, padded // MOE_BLK, jnp.zeros_like(count)], axis=1)
    tile_row = jnp.arange(n_sorted // MOE_ROWS, dtype=jnp.int32) * MOE_ROWS
    tile_group = jnp.minimum(jnp.sum((tile_row[:, None] >= region_end[None, :]).astype(jnp.int32), axis=1), ng - 1)
    n_used = (region_end[-1] // MOE_ROWS).reshape(1)
    return meta.astype(jnp.int32).reshape(-1, 1, META_LANES), tile_group.astype(jnp.int32), n_used.astype(jnp.int32)


def _moe_block_moves(meta_ref, make_copy, action):
    ng = N_EXPERT_GROUPS
    for g in range(ng):
        local0 = meta_ref[0, 0, g]
        sorted0 = meta_ref[0, 0, ng + g]
        nblk = meta_ref[0, 0, 2 * ng + g]

        def body(k, carry, local0=local0, sorted0=sorted0):
            local = pl.multiple_of(local0 + k * MOE_BLK, MOE_BLK)
            srt = pl.multiple_of(sorted0 + k * MOE_BLK, MOE_BLK)
            action(make_copy(local, srt))
            return carry

        lax.fori_loop(0, nblk, body, 0)


def _one_hot_bf16(mask):
    return jnp.where(mask, 1.0, 0.0).astype(BF16)


def _dispatch_kernel(meta_ref, prev_ref, prev2_ref, h_ref, gate_ref, gate_t_ref, upper_ref, init_ref, xs_ref,
                     buf, sem, *, rloc):
    del init_ref
    step = pl.program_id(0) * pl.num_programs(1) + pl.program_id(1)
    last = pl.num_programs(0) * pl.num_programs(1) - 1
    slot = step % 2
    ng = N_EXPERT_GROUPS
    tm = h_ref.shape[1]

    def copy_from(slot_):
        return lambda local, srt: pltpu.make_async_copy(
            buf.at[slot_, pl.ds(local, MOE_BLK), :], xs_ref.at[pl.ds(srt, MOE_BLK), :], sem.at[slot_])

    @pl.when(step >= 2)
    def _():
        _moe_block_moves(prev2_ref, copy_from(slot), lambda c: c.wait())

    gid = gate_t_ref[0, 0][GROUP_ROW:GROUP_ROW + 1, :]
    sub = lax.broadcasted_iota(jnp.int32, (SUBLANES, tm), 0)
    mine = sub == gid.astype(jnp.int32)
    seen = jnp.dot(_one_hot_bf16(mine), upper_ref[...], preferred_element_type=F32)
    start = jnp.zeros((SUBLANES, tm), jnp.int32)
    for g in range(ng):
        start = jnp.where(sub == g, meta_ref[0, 0, g], start)
    dest = jnp.sum(jnp.where(mine, seen + start.astype(F32), 0.0), axis=0, keepdims=True)
    pick = _one_hot_bf16(lax.broadcasted_iota(jnp.int32, (rloc, tm), 0) == dest.astype(jnp.int32))

    d = h_ref.shape[2]
    buf[slot, :, 0:d] = jnp.dot(pick, h_ref[0], preferred_element_type=F32)
    p0, p1, p2 = _split3_bf16(gate_ref[0])
    dot = functools.partial(jnp.dot, preferred_element_type=F32)
    buf[slot, :, d:] = dot(pick, p0) + dot(pick, p1) + dot(pick, p2)

    _moe_block_moves(meta_ref, copy_from(slot), lambda c: c.start())

    @pl.when(step == last)
    def _():
        _moe_block_moves(meta_ref, copy_from(slot), lambda c: c.wait())

        @pl.when(step >= 1)
        def _():
            _moe_block_moves(prev_ref, copy_from(1 - slot), lambda c: c.wait())


def _dispatch(meta, h2, gates, gates_t, upper, n_sorted, i0):
    b, s, d = h2.shape
    tm = gates_t.shape[3]
    nta = s // tm - i0
    rloc = _moe_local_rows(tm)
    width = d + LANES

    def meta_at(delta):
        return lambda bb, i: (jnp.maximum(bb * nta + i - delta, 0), 0, 0)

    row = lambda bb, i: (bb, i + i0, 0)
    return pl.pallas_call(
        functools.partial(_dispatch_kernel, rloc=rloc),
        out_shape=jax.ShapeDtypeStruct((n_sorted, width), F32),
        grid=(b, nta),
        in_specs=[pl.BlockSpec((1, 1, META_LANES), meta_at(0), memory_space=pltpu.SMEM),
                  pl.BlockSpec((1, 1, META_LANES), meta_at(1), memory_space=pltpu.SMEM),
                  pl.BlockSpec((1, 1, META_LANES), meta_at(2), memory_space=pltpu.SMEM),
                  pl.BlockSpec((1, tm, d), row),
                  pl.BlockSpec((1, tm, LANES), row),
                  pl.BlockSpec((1, 1, LANES, tm), lambda bb, i: (bb, i + i0, 0, 0)),
                  pl.BlockSpec((tm, tm), lambda bb, i: (0, 0)),
                  pl.BlockSpec(memory_space=pl.ANY)],
        out_specs=pl.BlockSpec(memory_space=pl.ANY),
        scratch_shapes=[pltpu.VMEM((2, rloc, width), F32), pltpu.SemaphoreType.DMA((2,))],
        input_output_aliases={7: 0},
        compiler_params=_params("arbitrary", "arbitrary"),
    )(meta, meta, meta, h2, gates, gates_t, upper, jnp.zeros((n_sorted, width), F32))


def _group_ffn_kernel(tile_group_ref, n_used_ref, xs_ref, wg_ref, wu_ref, wd_ref, o_ref):
    i = pl.program_id(0)
    d = o_ref.shape[1]

    @pl.when(i < n_used_ref[0])
    def _():
        x = xs_ref[:, 0:d].astype(BF16)
        gates = xs_ref[:, d:]
        lane = lax.broadcasted_iota(jnp.int32, (1, LANES), 1)
        first = tile_group_ref[i] * EXPERTS_PER_GROUP
        acc = None
        for k in range(EXPERTS_PER_GROUP):
            ge = jnp.sum(jnp.where(lane == first + k, gates, 0.0), axis=-1, keepdims=True)
            gp = jnp.dot(x, wg_ref[0, k], preferred_element_type=F32)
            up = jnp.dot(x, wu_ref[0, k], preferred_element_type=F32)
            he = (gp * jax.nn.sigmoid(gp)) * up * ge
            part = jnp.dot(he.astype(BF16), wd_ref[0, k], preferred_element_type=F32)
            acc = part if acc is None else acc + part
        o_ref[...] = acc

    @pl.when(i >= n_used_ref[0])
    def _():
        o_ref[...] = jnp.zeros_like(o_ref)


def _group_ffn(tile_group, n_used, xs, wg, wu, wd):
    n_sorted, width = xs.shape
    d = width - LANES
    ff = wg.shape[-1]
    epg = EXPERTS_PER_GROUP
    per_group = lambda i, tg, nu: (tg[i], 0, 0, 0)
    rows = lambda i, tg, nu: (i, 0)
    grid_spec = pltpu.PrefetchScalarGridSpec(
        num_scalar_prefetch=2,
        grid=(n_sorted // MOE_ROWS,),
        in_specs=[pl.BlockSpec((MOE_ROWS, width), rows),
                  pl.BlockSpec((1, epg, d, ff), per_group),
                  pl.BlockSpec((1, epg, d, ff), per_group),
                  pl.BlockSpec((1, epg, ff, d), per_group)],
        out_specs=pl.BlockSpec((MOE_ROWS, d), rows))
    return pl.pallas_call(
        _group_ffn_kernel,
        out_shape=jax.ShapeDtypeStruct((n_sorted, d), F32),
        grid_spec=grid_spec,
        compiler_params=_params("arbitrary"),
    )(tile_group, n_used, xs, wg, wu, wd)


def _combine_kernel(meta_ref, next_ref, gate_ref, x1_ref, mod_ref, lower_ref, ys_ref, o_ref, buf, sem):
    step = pl.program_id(0) * pl.num_programs(1) + pl.program_id(1)
    last = pl.num_programs(0) * pl.num_programs(1) - 1
    slot = step % 2
    ng = N_EXPERT_GROUPS
    rloc = buf.shape[1]

    def copy_into(slot_):
        return lambda local, srt: pltpu.make_async_copy(
            ys_ref.at[pl.ds(srt, MOE_BLK), :], buf.at[slot_, pl.ds(local, MOE_BLK), :], sem.at[slot_])

    @pl.when(step == 0)
    def _():
        buf[...] = jnp.zeros_like(buf)
        _moe_block_moves(meta_ref, copy_into(0), lambda c: c.start())

    @pl.when(step < last)
    def _():
        _moe_block_moves(next_ref, copy_into(1 - slot), lambda c: c.start())

    gates = gate_ref[0]
    gid = gates[:, GROUP_ROW:GROUP_ROW + 1].astype(jnp.int32)
    lane = lax.broadcasted_iota(jnp.int32, (1, LANES), 1)
    mine = lane == gid
    seen = jnp.dot(lower_ref[...], _one_hot_bf16(mine), preferred_element_type=F32)
    start = jnp.zeros((1, LANES), jnp.int32)
    for g in range(ng):
        start = jnp.where(lane == g, meta_ref[0, 0, g], start)
    dest = jnp.sum(jnp.where(mine, seen + start.astype(F32), 0.0), axis=-1, keepdims=True)
    pick = _one_hot_bf16(lax.broadcasted_iota(jnp.int32, (1, rloc), 1) == dest.astype(jnp.int32))

    _moe_block_moves(meta_ref, copy_into(slot), lambda c: c.wait())
    y_hi, y_lo = _split_bf16(buf[slot])
    y = (jnp.dot(pick, y_hi, preferred_element_type=F32) + jnp.dot(pick, y_lo, preferred_element_type=F32))
    o_ref[0] = x1_ref[0] + mod_ref[0, 0, 5:6, :] * y


def _combine(meta, gates, x1, modtab, lower, ys, i0, nct):
    b, s, d = x1.shape
    tm = lower.shape[0]
    nta = s // tm - i0
    n_meta = meta.shape[0]
    rloc = _moe_local_rows(tm)

    def meta_at(delta):
        return lambda bb, i: (jnp.minimum(bb * nta + i + delta, n_meta - 1), 0, 0)

    row = lambda bb, i: (bb, i + i0, 0)
    return pl.pallas_call(
        _combine_kernel,
        out_shape=jax.ShapeDtypeStruct((b, nta * tm, d), F32),
        grid=(b, nta),
        in_specs=[pl.BlockSpec((1, 1, META_LANES), meta_at(0), memory_space=pltpu.SMEM),
                  pl.BlockSpec((1, 1, META_LANES), meta_at(1), memory_space=pltpu.SMEM),
                  pl.BlockSpec((1, tm, LANES), row),
                  pl.BlockSpec((1, tm, d), row),
                  pl.BlockSpec((1, 1, 6, d), lambda bb, i: (bb, jnp.where(i + i0 >= nct, 1, 0), 0, 0)),
                  pl.BlockSpec((tm, tm), lambda bb, i: (0, 0)),
                  pl.BlockSpec(memory_space=pl.ANY)],
        out_specs=pl.BlockSpec((1, tm, d), lambda bb, i: (bb, i, 0)),
        scratch_shapes=[pltpu.VMEM((2, rloc, d), F32), pltpu.SemaphoreType.DMA((2,))],
        compiler_params=_params("arbitrary", "arbitrary"),
    )(meta, meta, gates, x1, modtab, lower, ys)


def _experts(h2, gates, gates_t, x1, modtab, wg, wu, wd, upper, lower, i0, nct):
    b, s, d = x1.shape
    tm = gates_t.shape[3]
    nta = s // tm - i0
    n_sorted = _moe_sorted_rows(b * nta, tm)
    group_ids = gates[:, i0 * tm:, GROUP_ROW].astype(jnp.int32).reshape(b * nta, tm)
    meta, tile_group, n_used = _moe_plan(group_ids, n_sorted)
    xs = _dispatch(meta, h2, gates, gates_t, upper, n_sorted, i0)
    ys = _group_ffn(tile_group, n_used, xs, wg, wu, wd)
    return _combine(meta, gates, x1, modtab, lower, ys, i0, nct)


def _rope_tables(l, lc):
    t = jnp.arange(l, dtype=jnp.int32)
    row = (t // GRID_W).astype(F32)
    col = (t % GRID_W).astype(F32)
    inv_freq = ROPE_BASE ** (-jnp.arange(ROPE_PAIRS_PER_AXIS, dtype=F32) / ROPE_PAIRS_PER_AXIS)
    ang = jnp.concatenate([row[:, None] * inv_freq, col[:, None] * inv_freq], axis=-1)
    cos, sin = jnp.cos(ang), jnp.sin(ang)
    zero = jnp.zeros_like(sin)
    cos_h = jnp.concatenate([cos, cos], axis=-1)
    sin_lo = jnp.concatenate([-sin, zero], axis=-1)
    sin_hi = jnp.concatenate([zero, sin], axis=-1)
    def full(tab, ctx_value):
        tab = jnp.tile(tab, (1, NA_HEADS))
        return jnp.concatenate([jnp.full((lc, NA_WIDTH), ctx_value, F32), tab], axis=0)
    return full(cos_h, 1.0), full(sin_lo, 0.0), full(sin_hi, 0.0)


def kernel(x, c, ctx, c_ctx, w_ada, b_ada, norm1_g, norm2_g, w_in, q_norm_g, k_norm_g, rpb, s5_lam_re, s5_lam_im, s5_log_step, s5_b_re, s5_b_im, s5_c_re, s5_c_im, s5_d, w_glu, b_glu, w_fnet, w_out, w_router, b_router, w_gate, w_up, w_down):
    b, l, d = x.shape
    lc = ctx.shape[1]
    s = lc + l
    depth = w_ada.shape[0]
    rows = l // GRID_W
    tm = min(256, lc)
    assert lc % tm == 0 and l % tm == 0 and l % GRID_W == 0
    nct = lc // tm
    rb = 4 if rows % 4 == 0 else 1
    tc = min(64, lc)
    nb = 2 if b % 2 == 0 else 1
    ident = jnp.eye(nb * tm, dtype=BF16)

    xa = jnp.concatenate([ctx, x], axis=1)

    n_mod = -(-(b + 1) // SUBLANES) * SUBLANES
    cvec = jnp.concatenate([c, c_ctx[None, :], jnp.zeros((n_mod - b - 1, d), F32)], axis=0)
    mod_all = _modulation(cvec, w_ada, b_ada)

    cos, sin_lo, sin_hi = _rope_tables(l, lc)
    head_ones = jnp.asarray(np.kron(np.eye(NA_HEADS), np.ones((HEAD_DIM, HEAD_DIM))), dtype=BF16)
    csl, csc, c64, s64 = _fnet_constants(l, lc)
    w_router_t = w_router.T
    earlier = np.triu(np.ones((tm, tm)), 1)
    upper = jnp.asarray(earlier, dtype=BF16)
    lower = jnp.asarray(earlier.T, dtype=BF16)
    by_group = lambda w: w.astype(BF16).reshape((N_EXPERT_GROUPS, EXPERTS_PER_GROUP) + w.shape[1:])

    for layer in range(depth):
        mod = mod_all[layer]
        modtab = jnp.stack([jnp.broadcast_to(mod[b].reshape(1, 6, d), (b, 6, d)),
                            mod[:b].reshape(b, 6, d)], axis=1)
        qg = jnp.tile(q_norm_g[layer] * ATTN_SCALE, NA_HEADS).reshape(1, NA_WIDTH)
        kg = jnp.tile(k_norm_g[layer], NA_HEADS).reshape(1, NA_WIDTH)
        q, k, v, us, uf = _in_projection(xa, modtab, norm1_g[layer].reshape(1, d), w_in[layer].astype(BF16),
                                         qg, kg, head_ones, cos, sin_lo, sin_hi, tm, nct, nb)

        bias = _attn_bias_tables(rpb[layer], rows, rb)
        att = _attention(q, k, v, bias, lc, rows, rb)

        bmat, cmat, a_re, a_im = _s5_matrices(s5_lam_re[layer], s5_lam_im[layer], s5_log_step[layer],
                                              s5_b_re[layer], s5_b_im[layer], s5_c_re[layer], s5_c_im[layer], b)
        y_fwd, y_bwd = _s5_scan(us, bmat, cmat, a_re, a_im, lc, tc)

        wbd = jnp.einsum('hde,hg->hdge', w_fnet[layer], jnp.eye(FNET_HEADS, dtype=F32))
        fou = _fourier(uf, csl, csc, c64, s64, wbd.reshape(FNET_WIDTH, FNET_WIDTH).astype(BF16), lc)

        x1, h2, gates, gates_t = _out_projection(
            xa, att, y_fwd, y_bwd, us, fou, modtab, s5_d[layer].reshape(1, S5_WIDTH),
            w_glu[layer].astype(BF16), b_glu[layer].reshape(1, S5_WIDTH), w_out[layer].astype(BF16),
            norm2_g[layer].reshape(1, d), w_router_t, b_router, ident, tm, nct, nb)

        i0 = nct if layer == depth - 1 else 0
        xa = _experts(h2, gates, gates_t, x1, modtab, by_group(w_gate[layer]), by_group(w_up[layer]),
                      by_group(w_down[layer]), upper, lower, i0, nct)

    return xa
```

```python
import functools
import math

import numpy as np
import jax
import jax.numpy as jnp
from jax import lax
from jax.experimental import pallas as pl
from jax.experimental.pallas import tpu as pltpu

F32 = jnp.float32
BF16 = jnp.bfloat16

GRID_W = 64
EPS = 1e-6
HEAD_DIM = 64
NA_WIDTH = 512
NA_HEADS = 8
NA_WIN_H = 8
NA_WIN_W = 16
S5_WIDTH = 256
S5_GROUP = 16
S5_GROUPS = 16
S5_STATE = 64
S5_COLS = S5_GROUPS * S5_STATE
FNET_WIDTH = 256
FNET_HEADS = 4
FNET_HEAD_DIM = 64
IN_COLS = 3 * NA_WIDTH + S5_WIDTH + FNET_WIDTH
ATTN_SCALE = HEAD_DIM ** -0.5
ROPE_BASE = 100.0
ROPE_PAIRS_PER_AXIS = HEAD_DIM // 4
N_EXPERTS = 16
GROUP_ROW = N_EXPERTS
N_EXPERT_GROUPS = 4
EXPERTS_PER_GROUP = 4
EXPERT_FF = 512

LANES = 128
SUBLANES = 8
VMEM_LIMIT = 56 * 1024 * 1024

NEG_BIG = -1e30
NT_DIMS = (((1,), (1,)), ((), ()))


def _split_bf16(x):
    hi = x.astype(BF16)
    lo = (x - hi.astype(F32)).astype(BF16)
    return hi, lo


def _split3_bf16(x):
    p0 = x.astype(BF16)
    r1 = x - p0.astype(F32)
    p1 = r1.astype(BF16)
    p2 = (r1 - p1.astype(F32)).astype(BF16)
    return p0, p1, p2


def _dot_f32(a, b, dims=None):
    a_hi, a_lo = _split_bf16(a)
    b_hi, b_lo = _split_bf16(b)
    if dims is None:
        dot = functools.partial(jnp.dot, preferred_element_type=F32)
    else:
        dot = functools.partial(lax.dot_general, dimension_numbers=dims, preferred_element_type=F32)
    return dot(a_hi, b_hi) + dot(a_hi, b_lo) + dot(a_lo, b_hi)


def _params(*sem):
    return pltpu.CompilerParams(dimension_semantics=sem, vmem_limit_bytes=VMEM_LIMIT)


def _mod_kernel(c_ref, w_ref, b_ref, o_ref):
    c = c_ref[...]
    s = c * jax.nn.sigmoid(c)
    o_ref[0] = _dot_f32(s, w_ref[0]) + b_ref[0]


def _modulation(cvec, w_ada, b_ada):
    depth, d, n = w_ada.shape
    r = cvec.shape[0]
    tn = 768
    return pl.pallas_call(
        _mod_kernel,
        out_shape=jax.ShapeDtypeStruct((depth, r, n), F32),
        grid=(depth, n // tn),
        in_specs=[pl.BlockSpec((r, d), lambda l, j: (0, 0)),
                  pl.BlockSpec((1, d, tn), lambda l, j: (l, 0, j)),
                  pl.BlockSpec((1, 1, tn), lambda l, j: (l, 0, j))],
        out_specs=pl.BlockSpec((1, r, tn), lambda l, j: (l, 0, j)),
        compiler_params=_params("arbitrary", "arbitrary"),
    )(cvec, w_ada, b_ada.reshape(depth, 1, n))


def _inproj_kernel(x_ref, mod_ref, g1_ref, w_ref, qg_ref, kg_ref, gm_ref, cos_ref, sa_ref, sb_ref,
                   q_ref, k_ref, v_ref, us_ref, uf_ref):
    cos = cos_ref[...]
    sin_lo = sa_ref[...]
    sin_hi = sb_ref[...]
    gm = gm_ref[...]
    nw = NA_WIDTH

    def head_norm_rope(t, g):
        ss = jnp.dot((t * t).astype(BF16), gm, preferred_element_type=F32)
        tn = t * lax.rsqrt(ss * (1.0 / HEAD_DIM) + EPS) * g
        half = HEAD_DIM // 2
        return (tn * cos + pltpu.roll(tn, nw - half, 1) * sin_lo + pltpu.roll(tn, half, 1) * sin_hi)

    for n in range(x_ref.shape[0]):
        x = x_ref[n]
        mod = mod_ref[n, 0]
        r = lax.rsqrt(jnp.mean(x * x, axis=-1, keepdims=True) + EPS)
        h = (x * r * g1_ref[...]) * (1.0 + mod[1:2]) + mod[0:1]
        proj = jnp.dot(h.astype(BF16), w_ref[...], preferred_element_type=F32)
        q_ref[n] = head_norm_rope(proj[:, 0:nw], qg_ref[...]).astype(BF16)
        k_ref[n] = head_norm_rope(proj[:, nw:2 * nw], kg_ref[...]).astype(BF16)
        v_ref[n] = proj[:, 2 * nw:3 * nw].astype(BF16)
        us_ref[n] = proj[:, 3 * nw:3 * nw + S5_WIDTH]
        uf_ref[n] = proj[:, 3 * nw + S5_WIDTH:].astype(BF16)


def _in_projection(xa, modtab, g1, w_in_bf, qg, kg, gm, cos, sin_lo, sin_hi, tm, nct, nb):
    b, s, d = xa.shape
    nt = s // tm
    nw = NA_WIDTH
    row = lambda i, bb: (bb, i, 0)
    const2 = lambda i, bb: (0, 0)
    tab = lambda i, bb: (i, 0)
    return pl.pallas_call(
        _inproj_kernel,
        out_shape=(jax.ShapeDtypeStruct((b, s, nw), BF16),
                   jax.ShapeDtypeStruct((b, s, nw), BF16),
                   jax.ShapeDtypeStruct((b, s, nw), BF16),
                   jax.ShapeDtypeStruct((b, s, S5_WIDTH), F32),
                   jax.ShapeDtypeStruct((b, s, FNET_WIDTH), BF16)),
        grid=(nt, b // nb),
        in_specs=[pl.BlockSpec((nb, tm, d), row),
                  pl.BlockSpec((nb, 1, 6, d), lambda i, bb: (bb, jnp.where(i >= nct, 1, 0), 0, 0)),
                  pl.BlockSpec((1, d), const2),
                  pl.BlockSpec((d, IN_COLS), const2),
                  pl.BlockSpec((1, nw), const2),
                  pl.BlockSpec((1, nw), const2),
                  pl.BlockSpec((nw, nw), const2),
                  pl.BlockSpec((tm, nw), tab),
                  pl.BlockSpec((tm, nw), tab),
                  pl.BlockSpec((tm, nw), tab)],
        out_specs=(pl.BlockSpec((nb, tm, nw), row),
                   pl.BlockSpec((nb, tm, nw), row),
                   pl.BlockSpec((nb, tm, nw), row),
                   pl.BlockSpec((nb, tm, S5_WIDTH), row),
                   pl.BlockSpec((nb, tm, FNET_WIDTH), row)),
        compiler_params=_params("arbitrary", "arbitrary"),
    )(xa, modtab, g1, w_in_bf, qg, kg, gm, cos, sin_lo, sin_hi)


def _attn_blocks(rows, rb):
    kh = min(NA_WIN_H, rows)
    kr = min(rb + kh - 1, rows)
    blocks, classes = [], {}
    for r0 in range(0, rows, rb):
        ks = min(max(r0 - kh // 2, 0), rows - kr)
        starts = tuple(min(max(r - kh // 2, 0), rows - kh) - ks for r in range(r0, r0 + rb))
        key = (r0 - ks, starts)
        cls = classes.setdefault(key, len(classes))
        blocks.append((r0, ks, cls))
    return kh, kr, blocks, list(classes.keys())


def _attn_bias_tables(rpb, rows, rb):
    kh, kr, blocks, class_keys = _attn_blocks(rows, rb)
    w = GRID_W
    col = np.arange(w)
    col_start = np.clip(col - NA_WIN_W // 2, 0, w - NA_WIN_W)
    n_row, n_col = 2 * NA_WIN_H - 1, 2 * NA_WIN_W - 1
    col_off = np.clip(col[None, :] - col[:, None] + (NA_WIN_W - 1), 0, n_col - 1)
    col_pick = jnp.asarray(np.eye(n_col)[col_off].reshape(w * w, n_col), dtype=F32)
    col_valid = (col[None, :] >= col_start[:, None]) & (col[None, :] < col_start[:, None] + NA_WIN_W)
    tabs = []
    for q_minus_ks, starts in class_keys:
        rq = np.arange(rb)[:, None]
        i = np.arange(kr)[None, :]
        rs = np.asarray(starts)[:, None]
        row_valid = (i >= rs) & (i < rs + kh)
        row_off = np.clip(i - (q_minus_ks + rq) + (NA_WIN_H - 1), 0, n_row - 1)
        row_pick = jnp.asarray(np.eye(n_row)[row_off].reshape(rb * kr, n_row), dtype=F32)
        bias = jnp.einsum('hyx,ny,mx->hnm', rpb.astype(F32), row_pick, col_pick,
                          precision=lax.Precision.HIGHEST)
        bias = bias.reshape(-1, rb, kr, w, w).transpose(0, 1, 3, 2, 4)
        valid = row_valid[:, None, :, None] & col_valid[None, :, None, :]
        tabs.append(jnp.where(valid[None], bias, NEG_BIG).reshape(-1, rb * w, kr * w))
    return jnp.stack(tabs, axis=0)


def _attn_kernel(q_ref, k_ref, v_ref, bias_ref, o_ref, *, lc, rb, kr, blocks):
    lane = lax.broadcasted_iota(jnp.int32, (1, LANES), 1)
    first_head = lane < HEAD_DIM

    def stack_heads(qb):
        z = jnp.zeros_like(qb)
        return jnp.concatenate([jnp.where(first_head, qb, z), jnp.where(first_head, z, qb)], axis=0)

    def unstack_heads(o, n):
        return jnp.where(first_head, o[:n], o[n:])

    kc = k_ref[0, 0:lc, :]
    vc = v_ref[0, 0:lc, :]

    s = lax.dot_general(stack_heads(q_ref[0, 0:lc, :]), kc, NT_DIMS, preferred_element_type=F32)
    m = jnp.max(s, axis=-1, keepdims=True)
    p = jnp.exp(s - m)
    den = jnp.sum(p, axis=-1, keepdims=True)
    o = jnp.dot(p.astype(BF16), vc, preferred_element_type=F32) / den
    o_ref[0, 0:lc, :] = unstack_heads(o, lc).astype(BF16)

    nq = rb * GRID_W
    nk = kr * GRID_W
    for r0, ks, cls in blocks:
        q0 = lc + r0 * GRID_W
        k0 = lc + ks * GRID_W
        q2 = stack_heads(q_ref[0, q0:q0 + nq, :])
        k_loc = k_ref[0, k0:k0 + nk, :]
        v_loc = v_ref[0, k0:k0 + nk, :]
        s_loc = lax.dot_general(q2, k_loc, NT_DIMS, preferred_element_type=F32)
        s_loc = s_loc + bias_ref[cls].reshape(2 * nq, nk)
        s_ctx = lax.dot_general(q2, kc, NT_DIMS, preferred_element_type=F32)
        m = jnp.maximum(jnp.max(s_loc, axis=-1, keepdims=True), jnp.max(s_ctx, axis=-1, keepdims=True))
        p_loc = jnp.exp(s_loc - m)
        p_ctx = jnp.exp(s_ctx - m)
        den = jnp.sum(p_loc, axis=-1, keepdims=True) + jnp.sum(p_ctx, axis=-1, keepdims=True)
        o = (jnp.dot(p_loc.astype(BF16), v_loc, preferred_element_type=F32)
             + jnp.dot(p_ctx.astype(BF16), vc, preferred_element_type=F32)) / den
        o_ref[0, q0:q0 + nq, :] = unstack_heads(o, nq).astype(BF16)


def _attention(q, k, v, bias, lc, rows, rb):
    b, s, nw = q.shape
    _, kr, blocks, _ = _attn_blocks(rows, rb)
    ncls = bias.shape[0]
    spec = pl.BlockSpec((1, s, LANES), lambda bb, p: (bb, 0, p))
    return pl.pallas_call(
        functools.partial(_attn_kernel, lc=lc, rb=rb, kr=kr, blocks=blocks),
        out_shape=jax.ShapeDtypeStruct((b, s, nw), BF16),
        grid=(b, nw // LANES),
        in_specs=[spec, spec, spec,
                  pl.BlockSpec((ncls, 2, rb * GRID_W, kr * GRID_W), lambda bb, p: (0, p, 0, 0))],
        out_specs=spec,
        compiler_params=_params("arbitrary", "arbitrary"),
    )(q, k, v, bias)


def _s5_matrices(lam_re, lam_im, log_step, b_re, b_im, c_re, c_im, batch):
    lam_re = lam_re.astype(F32)
    lam_im = lam_im.astype(F32)
    step = jnp.exp(log_step.astype(F32))[..., None]
    mag = jnp.exp(lam_re * step)
    a_re = mag * jnp.cos(lam_im * step)
    a_im = mag * jnp.sin(lam_im * step)
    den = lam_re * lam_re + lam_im * lam_im
    co_re = ((a_re - 1.0) * lam_re + a_im * lam_im) / den
    co_im = (a_im * lam_re - (a_re - 1.0) * lam_im) / den
    b_re = b_re.astype(F32)
    b_im = b_im.astype(F32)
    bb_re = co_re[..., None] * b_re - co_im[..., None] * b_im
    bb_im = co_re[..., None] * b_im + co_im[..., None] * b_re
    eye = jnp.eye(S5_GROUPS, dtype=F32)
    to_in = lambda t: jnp.einsum('dgpc,gh->dgchp', t, eye).reshape(2, S5_WIDTH, S5_COLS)
    bmat = jnp.concatenate([to_in(bb_re), to_in(bb_im)], axis=-1)
    to_out = lambda t: jnp.einsum('dgcp,gh->dgphc', t.astype(F32), eye).reshape(2, S5_COLS, S5_WIDTH)
    cmat = jnp.concatenate([to_out(c_re), -to_out(c_im)], axis=1)
    bc = lambda t: jnp.broadcast_to(t.reshape(2, 1, S5_COLS), (2, batch, S5_COLS))
    return bmat.astype(BF16), cmat.astype(BF16), bc(a_re), bc(a_im)


def _s5_direction(u_ref, bm_ref, cm_ref, ar_ref, ai_ref, y_ref, hbuf, st_re, st_im, tc, batch, reverse):
    u = jnp.swapaxes(u_ref[...], 0, 1).reshape(tc * batch, S5_WIDTH)
    hbuf[...] = jnp.dot(u.astype(BF16), bm_ref[0], preferred_element_type=F32)

    ncol = 2 * LANES
    for cg in range(S5_COLS // ncol):
        re = slice(cg * ncol, (cg + 1) * ncol)
        im = slice(S5_COLS + cg * ncol, S5_COLS + (cg + 1) * ncol)
        h_re = st_re[:, re]
        h_im = st_im[:, re]
        for tt in range(tc):
            t = tc - 1 - tt if reverse else tt
            rows = slice(t * batch, (t + 1) * batch)
            a_re = ar_ref[0, :, re]
            a_im = ai_ref[0, :, re]
            n_re = a_re * h_re - a_im * h_im + hbuf[rows, re]
            n_im = a_re * h_im + a_im * h_re + hbuf[rows, im]
            hbuf[rows, re] = n_re
            hbuf[rows, im] = n_im
            h_re, h_im = n_re, n_im
        st_re[:, re] = h_re
        st_im[:, re] = h_im

    y = jnp.dot(hbuf[...].astype(BF16), cm_ref[0], preferred_element_type=F32)
    y_ref[...] = jnp.swapaxes(y.reshape(tc, batch, S5_WIDTH), 0, 1)


def _s5_kernel(uf_ref, ub_ref, bmf_ref, bmb_ref, cmf_ref, cmb_ref, arf_ref, arb_ref, aif_ref, aib_ref,
               yf_ref, yb_ref, hbuf_f, hbuf_b, sf_re, sf_im, sb_re, sb_im, *, tc, batch):
    @pl.when(pl.program_id(0) == 0)
    def _():
        for st in (sf_re, sf_im, sb_re, sb_im):
            st[...] = jnp.zeros_like(st)

    _s5_direction(uf_ref, bmf_ref, cmf_ref, arf_ref, aif_ref, yf_ref, hbuf_f, sf_re, sf_im, tc, batch, False)
    _s5_direction(ub_ref, bmb_ref, cmb_ref, arb_ref, aib_ref, yb_ref, hbuf_b, sb_re, sb_im, tc, batch, True)


def _s5_scan(us, bmat, cmat, a_re, a_im, lc, tc):
    batch, s, w = us.shape
    nch = s // tc
    nch_c = lc // tc
    fwd = lambda j: (0, j, 0)
    bwd = lambda j: (0, jnp.where(j < nch_c, nch_c - 1 - j, nch + nch_c - 1 - j), 0)
    dir_f = lambda j: (0, 0, 0)
    dir_b = lambda j: (1, 0, 0)
    hbuf = pltpu.VMEM((tc * batch, 2 * S5_COLS), F32)
    state = pltpu.VMEM((batch, S5_COLS), F32)
    out = jax.ShapeDtypeStruct((batch, s, w), F32)
    return pl.pallas_call(
        functools.partial(_s5_kernel, tc=tc, batch=batch),
        out_shape=(out, out),
        grid=(nch,),
        in_specs=[pl.BlockSpec((batch, tc, w), fwd),
                  pl.BlockSpec((batch, tc, w), bwd),
                  pl.BlockSpec((1, w, 2 * S5_COLS), dir_f),
                  pl.BlockSpec((1, w, 2 * S5_COLS), dir_b),
                  pl.BlockSpec((1, 2 * S5_COLS, w), dir_f),
                  pl.BlockSpec((1, 2 * S5_COLS, w), dir_b),
                  pl.BlockSpec((1, batch, S5_COLS), dir_f),
                  pl.BlockSpec((1, batch, S5_COLS), dir_b),
                  pl.BlockSpec((1, batch, S5_COLS), dir_f),
                  pl.BlockSpec((1, batch, S5_COLS), dir_b)],
        out_specs=(pl.BlockSpec((batch, tc, w), fwd), pl.BlockSpec((batch, tc, w), bwd)),
        scratch_shapes=[hbuf, hbuf, state, state, state, state],
        compiler_params=_params("arbitrary"),
    )(us, us, bmat, bmat, cmat, cmat, a_re, a_re, a_im, a_im)


def _dft_cos_sin(n):
    k = np.arange(n, dtype=np.int64)
    ang = 2.0 * np.pi * ((k[:, None] * k[None, :]) % n).astype(np.float64) / n
    return np.cos(ang), np.sin(ang)


def _fnet_constants(l, lc):
    def seq_matrix(n):
        c, s = _dft_cos_sin(n)
        scale = 1.0 / math.sqrt(n * FNET_HEAD_DIM)
        return jnp.asarray(np.concatenate([c, -s], axis=1) * scale, dtype=F32).astype(BF16)
    c64, s64 = _dft_cos_sin(FNET_HEAD_DIM)
    eye = np.eye(FNET_HEADS)
    return (seq_matrix(l), seq_matrix(lc),
            jnp.asarray(np.kron(eye, c64), dtype=F32).astype(BF16),
            jnp.asarray(np.kron(eye, s64), dtype=F32).astype(BF16))


def _fnet_kernel(u_ref, csl_ref, csc_ref, c64_ref, s64_ref, w_ref, o_ref, tl_ref, tc_ref, *, lc, l):
    i = pl.program_id(1)
    w = w_ref[...]
    nb, _, width = u_ref.shape

    def finish(z):
        for n in range(nb):
            zn = z[:, n * width:(n + 1) * width].astype(BF16)
            o_ref[n] = jnp.dot(zn, w, preferred_element_type=F32).astype(BF16)

    @pl.when(i == 0)
    def _():
        for n in range(nb):
            u = u_ref[n]
            cols = slice(n * width, (n + 1) * width)
            t_cos = jnp.dot(u, c64_ref[...], preferred_element_type=F32).astype(BF16)
            t_sin = jnp.dot(u, s64_ref[...], preferred_element_type=F32).astype(BF16)
            tc_ref[0:lc, cols] = t_cos[0:lc]
            tc_ref[lc:2 * lc, cols] = t_sin[0:lc]
            tl_ref[0:l, cols] = t_cos[lc:]
            tl_ref[l:2 * l, cols] = t_sin[lc:]
        finish(jnp.dot(csc_ref[...], tc_ref[...], preferred_element_type=F32))

    @pl.when(i > 0)
    def _():
        finish(jnp.dot(csl_ref[...], tl_ref[...], preferred_element_type=F32))


def _fourier(uf, csl, csc, c64, s64, wbd, lc, nb):
    b, s, w = uf.shape
    l = s - lc
    tmf = lc
    nt = 1 + l // tmf
    const = lambda bb, i: (0, 0)
    return pl.pallas_call(
        functools.partial(_fnet_kernel, lc=lc, l=l),
        out_shape=jax.ShapeDtypeStruct((b, s, w), BF16),
        grid=(b // nb, nt),
        in_specs=[pl.BlockSpec((nb, s, w), lambda bb, i: (bb, 0, 0)),
                  pl.BlockSpec((tmf, 2 * l), lambda bb, i: (jnp.maximum(i - 1, 0), 0)),
                  pl.BlockSpec((lc, 2 * lc), const),
                  pl.BlockSpec((w, w), const),
                  pl.BlockSpec((w, w), const),
                  pl.BlockSpec((w, w), const)],
        out_specs=pl.BlockSpec((nb, tmf, w), lambda bb, i: (bb, i, 0)),
        scratch_shapes=[pltpu.VMEM((2 * l, nb * w), BF16), pltpu.VMEM((2 * lc, nb * w), BF16)],
        compiler_params=_params("arbitrary", "arbitrary"),
    )(uf, csl, csc, c64, s64, wbd)


def _gelu_tanh(x):
    return 0.5 * x * (1.0 + jnp.tanh(math.sqrt(2.0 / math.pi) * (x + 0.044715 * (x * x * x))))


def _route(logits_t, b_router_ref):
    e_all = range(N_EXPERTS)
    aff = [jax.nn.sigmoid(logits_t[e:e + 1, :]) for e in e_all]
    sel = [aff[e] + b_router_ref[e] for e in e_all]
    epg = EXPERTS_PER_GROUP
    keep = []
    for e in e_all:
        g0 = (e // epg) * epg
        rank = jnp.zeros_like(sel[e], dtype=jnp.int32)
        for o in range(g0, g0 + epg):
            if o == e:
                continue
            ahead = (sel[o] >= sel[e]) if o < e else (sel[o] > sel[e])
            rank = rank + ahead.astype(jnp.int32)
        keep.append(rank < 2)
    score = []
    for g in range(N_EXPERT_GROUPS):
        s0, s1, s2, s3 = sel[g * epg:(g + 1) * epg]
        hi01, lo01 = jnp.maximum(s0, s1), jnp.minimum(s0, s1)
        hi23, lo23 = jnp.maximum(s2, s3), jnp.minimum(s2, s3)
        top1 = jnp.maximum(hi01, hi23)
        top2 = jnp.maximum(jnp.minimum(hi01, hi23), jnp.maximum(lo01, lo23))
        score.append(top1 + top2)
    chosen = []
    for g in range(N_EXPERT_GROUPS):
        best = None
        for o in range(N_EXPERT_GROUPS):
            if o == g:
                continue
            cond = (score[g] > score[o]) if o < g else (score[g] >= score[o])
            best = cond if best is None else (best & cond)
        chosen.append(best)
    picked = [jnp.where(chosen[e // epg] & keep[e], aff[e], 0.0) for e in e_all]
    total = picked[0]
    for e in range(1, N_EXPERTS):
        total = total + picked[e]
    group = jnp.zeros_like(total)
    for g in range(1, N_EXPERT_GROUPS):
        group = jnp.where(chosen[g], float(g), group)
    return [p / total for p in picked], group


def _outproj_kernel(x_ref, att_ref, yf_ref, yb_ref, us_ref, fou_ref, mod_ref, sd_ref, wglu_ref, bglu_ref,
                    wout_ref, g2_ref, wr_ref, br_ref, ident_ref,
                    x1_ref, h2_ref, gate_ref, gate_t_ref):
    nb, tm = x_ref.shape[0], x_ref.shape[1]
    h2_all = []
    for n in range(nb):
        mod = mod_ref[n, 0]
        y = sd_ref[...] * us_ref[n] + yf_ref[n] + yb_ref[n]
        g = _gelu_tanh(y)
        ssm = g * jax.nn.sigmoid(jnp.dot(g.astype(BF16), wglu_ref[...], preferred_element_type=F32)
                                 + bglu_ref[...])
        a0, a1 = NA_WIDTH, NA_WIDTH + S5_WIDTH
        o = (jnp.dot(att_ref[n], wout_ref[0:a0, :], preferred_element_type=F32)
             + jnp.dot(ssm.astype(BF16), wout_ref[a0:a1, :], preferred_element_type=F32)
             + jnp.dot(fou_ref[n], wout_ref[a1:, :], preferred_element_type=F32))
        x1 = x_ref[n] + mod[2:3] * o
        x1_ref[n] = x1
        r = lax.rsqrt(jnp.mean(x1 * x1, axis=-1, keepdims=True) + EPS)
        h2 = (x1 * r * g2_ref[...]) * (1.0 + mod[4:5]) + mod[3:4]
        h2_ref[n] = h2.astype(BF16)
        h2_all.append(h2)

    h2 = jnp.concatenate(h2_all, axis=0) if nb > 1 else h2_all[0]
    nt = nb * tm
    logits_t = _dot_f32(wr_ref[...], h2, NT_DIMS)
    gates, group = _route(logits_t, br_ref)
    sub = lax.broadcasted_iota(jnp.int32, (LANES, nt), 0)
    gt = jnp.zeros((LANES, nt), F32)
    for e in range(N_EXPERTS):
        gt = jnp.where(sub == e, gates[e], gt)
    gt = jnp.where(sub == GROUP_ROW, group, gt)
    p0, p1, p2 = _split3_bf16(gt)
    ident = ident_ref[...]
    dot_nt = functools.partial(lax.dot_general, dimension_numbers=NT_DIMS, preferred_element_type=F32)
    gate = dot_nt(ident, p0) + dot_nt(ident, p1) + dot_nt(ident, p2)
    for n in range(nb):
        gate_t_ref[n, 0] = gt[:, n * tm:(n + 1) * tm]
        gate_ref[n] = gate[n * tm:(n + 1) * tm, :]


def _out_projection(xa, att, y_fwd, y_bwd, us, fou, modtab, s5_d, w_glu_bf, b_glu, w_out_bf, g2, w_router_t, b_router,
                    ident, tm, nct, nb):
    b, s, d = xa.shape
    nt = s // tm
    row = lambda i, bb: (bb, i, 0)
    const2 = lambda i, bb: (0, 0)
    return pl.pallas_call(
        _outproj_kernel,
        out_shape=(jax.ShapeDtypeStruct((b, s, d), F32),
                   jax.ShapeDtypeStruct((b, s, d), BF16),
                   jax.ShapeDtypeStruct((b, s, LANES), F32),
                   jax.ShapeDtypeStruct((b, nt, LANES, tm), F32)),
        grid=(nt, b // nb),
        in_specs=[pl.BlockSpec((nb, tm, d), row),
                  pl.BlockSpec((nb, tm, NA_WIDTH), row),
                  pl.BlockSpec((nb, tm, S5_WIDTH), row),
                  pl.BlockSpec((nb, tm, S5_WIDTH), row),
                  pl.BlockSpec((nb, tm, S5_WIDTH), row),
                  pl.BlockSpec((nb, tm, FNET_WIDTH), row),
                  pl.BlockSpec((nb, 1, 6, d), lambda i, bb: (bb, jnp.where(i >= nct, 1, 0), 0, 0)),
                  pl.BlockSpec((1, S5_WIDTH), const2),
                  pl.BlockSpec((S5_WIDTH, S5_WIDTH), const2),
                  pl.BlockSpec((1, S5_WIDTH), const2),
                  pl.BlockSpec((d, d), const2),
                  pl.BlockSpec((1, d), const2),
                  pl.BlockSpec((N_EXPERTS, d), const2),
                  pl.BlockSpec(memory_space=pltpu.SMEM),
                  pl.BlockSpec((nb * tm, nb * tm), const2)],
        out_specs=(pl.BlockSpec((nb, tm, d), row),
                   pl.BlockSpec((nb, tm, d), row),
                   pl.BlockSpec((nb, tm, LANES), row),
                   pl.BlockSpec((nb, 1, LANES, tm), lambda i, bb: (bb, i, 0, 0))),
        compiler_params=_params("arbitrary", "arbitrary"),
    )(xa, att, y_fwd, y_bwd, us, fou, modtab, s5_d, w_glu_bf, b_glu, w_out_bf, g2, w_router_t, b_router, ident)


MOE_BLK = SUBLANES
MOE_ROWS = 512
META_LANES = 16


def _moe_local_rows(tm):
    need = tm + N_EXPERT_GROUPS * (MOE_BLK - 1)
    return -(-need // LANES) * LANES


def _moe_sorted_rows(n_tiles, tm):
    need = n_tiles * (tm + N_EXPERT_GROUPS * (MOE_BLK - 1))
    return -(-need // MOE_ROWS) * MOE_ROWS + N_EXPERT_GROUPS * MOE_ROWS


def _moe_plan(group_ids, n_sorted):
    ng = N_EXPERT_GROUPS
    onehot = group_ids[:, :, None] == jnp.arange(ng, dtype=jnp.int32)[None, None, :]
    count = jnp.sum(onehot.astype(jnp.int32), axis=1)
    padded = (count + (MOE_BLK - 1)) // MOE_BLK * MOE_BLK
    local_start = jnp.cumsum(padded, axis=1) - padded
    before = jnp.cumsum(padded, axis=0) - padded
    region = (jnp.sum(padded, axis=0) + (MOE_ROWS - 1)) // MOE_ROWS * MOE_ROWS
    region_end = jnp.cumsum(region)
    offset = (region_end - region)[None, :] + before
    meta = jnp.concatenate([local_start, offset, padded // MOE_BLK, jnp.zeros_like(count)], axis=1)
    tile_row = jnp.arange(n_sorted // MOE_ROWS, dtype=jnp.int32) * MOE_ROWS
    tile_group = jnp.minimum(jnp.sum((tile_row[:, None] >= region_end[None, :]).astype(jnp.int32), axis=1), ng - 1)
    n_used = (region_end[-1] // MOE_ROWS).reshape(1)
    return meta.astype(jnp.int32).reshape(-1, 1, META_LANES), tile_group.astype(jnp.int32), n_used.astype(jnp.int32)


def _moe_block_moves(meta_ref, make_copy, action):
    ng = N_EXPERT_GROUPS
    for g in range(ng):
        local0 = meta_ref[0, 0, g]
        sorted0 = meta_ref[0, 0, ng + g]
        nblk = meta_ref[0, 0, 2 * ng + g]

        def body(k, carry, local0=local0, sorted0=sorted0):
            local = pl.multiple_of(local0 + k * MOE_BLK, MOE_BLK)
            srt = pl.multiple_of(sorted0 + k * MOE_BLK, MOE_BLK)
            action(make_copy(local, srt))
            return carry

        lax.fori_loop(0, nblk, body, 0)


def _one_hot_bf16(mask):
    return jnp.where(mask, 1.0, 0.0).astype(BF16)


def _dispatch_kernel(meta_ref, prev_ref, prev2_ref, h_ref, gate_ref, gate_t_ref, upper_ref, init_ref, xs_ref,
                     buf, sem, *, rloc):
    del init_ref
    step = pl.program_id(0) * pl.num_programs(1) + pl.program_id(1)
    last = pl.num_programs(0) * pl.num_programs(1) - 1
    slot = step % 2
    ng = N_EXPERT_GROUPS
    tm = h_ref.shape[1]

    def copy_from(slot_):
        return lambda local, srt: pltpu.make_async_copy(
            buf.at[slot_, pl.ds(local, MOE_BLK), :], xs_ref.at[pl.ds(srt, MOE_BLK), :], sem.at[slot_])

    @pl.when(step >= 2)
    def _():
        _moe_block_moves(prev2_ref, copy_from(slot), lambda c: c.wait())

    gid = gate_t_ref[0, 0][GROUP_ROW:GROUP_ROW + 1, :]
    sub = lax.broadcasted_iota(jnp.int32, (SUBLANES, tm), 0)
    mine = sub == gid.astype(jnp.int32)
    seen = jnp.dot(_one_hot_bf16(mine), upper_ref[...], preferred_element_type=F32)
    start = jnp.zeros((SUBLANES, tm), jnp.int32)
    for g in range(ng):
        start = jnp.where(sub == g, meta_ref[0, 0, g], start)
    dest = jnp.sum(jnp.where(mine, seen + start.astype(F32), 0.0), axis=0, keepdims=True)
    pick = _one_hot_bf16(lax.broadcasted_iota(jnp.int32, (rloc, tm), 0) == dest.astype(jnp.int32))

    d = h_ref.shape[2]
    buf[slot, :, 0:d] = jnp.dot(pick, h_ref[0], preferred_element_type=F32)
    p0, p1, p2 = _split3_bf16(gate_ref[0])
    dot = functools.partial(jnp.dot, preferred_element_type=F32)
    buf[slot, :, d:] = dot(pick, p0) + dot(pick, p1) + dot(pick, p2)

    _moe_block_moves(meta_ref, copy_from(slot), lambda c: c.start())

    @pl.when(step == last)
    def _():
        _moe_block_moves(meta_ref, copy_from(slot), lambda c: c.wait())

        @pl.when(step >= 1)
        def _():
            _moe_block_moves(prev_ref, copy_from(1 - slot), lambda c: c.wait())


def _dispatch(meta, h2, gates, gates_t, upper, n_sorted, i0):
    b, s, d = h2.shape
    tm = gates_t.shape[3]
    nta = s // tm - i0
    rloc = _moe_local_rows(tm)
    width = d + LANES

    def meta_at(delta):
        return lambda bb, i: (jnp.maximum(bb * nta + i - delta, 0), 0, 0)

    row = lambda bb, i: (bb, i + i0, 0)
    return pl.pallas_call(
        functools.partial(_dispatch_kernel, rloc=rloc),
        out_shape=jax.ShapeDtypeStruct((n_sorted, width), F32),
        grid=(b, nta),
        in_specs=[pl.BlockSpec((1, 1, META_LANES), meta_at(0), memory_space=pltpu.SMEM),
                  pl.BlockSpec((1, 1, META_LANES), meta_at(1), memory_space=pltpu.SMEM),
                  pl.BlockSpec((1, 1, META_LANES), meta_at(2), memory_space=pltpu.SMEM),
                  pl.BlockSpec((1, tm, d), row),
                  pl.BlockSpec((1, tm, LANES), row),
                  pl.BlockSpec((1, 1, LANES, tm), lambda bb, i: (bb, i + i0, 0, 0)),
                  pl.BlockSpec((tm, tm), lambda bb, i: (0, 0)),
                  pl.BlockSpec(memory_space=pl.ANY)],
        out_specs=pl.BlockSpec(memory_space=pl.ANY),
        scratch_shapes=[pltpu.VMEM((2, rloc, width), F32), pltpu.SemaphoreType.DMA((2,))],
        input_output_aliases={7: 0},
        compiler_params=_params("arbitrary", "arbitrary"),
    )(meta, meta, meta, h2, gates, gates_t, upper, jnp.zeros((n_sorted, width), F32))


def _group_ffn_kernel(tile_group_ref, n_used_ref, xs_ref, wg_ref, wu_ref, wd_ref, o_ref):
    i = pl.program_id(0)
    d = o_ref.shape[1]

    @pl.when(i < n_used_ref[0])
    def _():
        x = xs_ref[:, 0:d].astype(BF16)
        gates = xs_ref[:, d:]
        lane = lax.broadcasted_iota(jnp.int32, (1, LANES), 1)
        first = tile_group_ref[i] * EXPERTS_PER_GROUP
        acc = None
        for k in range(EXPERTS_PER_GROUP):
            ge = jnp.sum(jnp.where(lane == first + k, gates, 0.0), axis=-1, keepdims=True)
            gp = jnp.dot(x, wg_ref[0, k], preferred_element_type=F32)
            up = jnp.dot(x, wu_ref[0, k], preferred_element_type=F32)
            he = (gp * jax.nn.sigmoid(gp)) * up * ge
            part = jnp.dot(he.astype(BF16), wd_ref[0, k], preferred_element_type=F32)
            acc = part if acc is None else acc + part
        o_ref[...] = acc

    @pl.when(i >= n_used_ref[0])
    def _():
        o_ref[...] = jnp.zeros_like(o_ref)


def _group_ffn(tile_group, n_used, xs, wg, wu, wd):
    n_sorted, width = xs.shape
    d = width - LANES
    ff = wg.shape[-1]
    epg = EXPERTS_PER_GROUP
    per_group = lambda i, tg, nu: (tg[i], 0, 0, 0)
    rows = lambda i, tg, nu: (i, 0)
    grid_spec = pltpu.PrefetchScalarGridSpec(
        num_scalar_prefetch=2,
        grid=(n_sorted // MOE_ROWS,),
        in_specs=[pl.BlockSpec((MOE_ROWS, width), rows),
                  pl.BlockSpec((1, epg, d, ff), per_group),
                  pl.BlockSpec((1, epg, d, ff), per_group),
                  pl.BlockSpec((1, epg, ff, d), per_group)],
        out_specs=pl.BlockSpec((MOE_ROWS, d), rows))
    return pl.pallas_call(
        _group_ffn_kernel,
        out_shape=jax.ShapeDtypeStruct((n_sorted, d), F32),
        grid_spec=grid_spec,
        compiler_params=_params("arbitrary"),
    )(tile_group, n_used, xs, wg, wu, wd)


def _combine_kernel(meta_ref, next_ref, gate_ref, x1_ref, mod_ref, lower_ref, ys_ref, o_ref, buf, sem):
    step = pl.program_id(0) * pl.num_programs(1) + pl.program_id(1)
    last = pl.num_programs(0) * pl.num_programs(1) - 1
    slot = step % 2
    ng = N_EXPERT_GROUPS
    rloc = buf.shape[1]

    def copy_into(slot_):
        return lambda local, srt: pltpu.make_async_copy(
            ys_ref.at[pl.ds(srt, MOE_BLK), :], buf.at[slot_, pl.ds(local, MOE_BLK), :], sem.at[slot_])

    @pl.when(step == 0)
    def _():
        buf[...] = jnp.zeros_like(buf)
        _moe_block_moves(meta_ref, copy_into(0), lambda c: c.start())

    @pl.when(step < last)
    def _():
        _moe_block_moves(next_ref, copy_into(1 - slot), lambda c: c.start())

    gates = gate_ref[0]
    gid = gates[:, GROUP_ROW:GROUP_ROW + 1].astype(jnp.int32)
    lane = lax.broadcasted_iota(jnp.int32, (1, LANES), 1)
    mine = lane == gid
    seen = jnp.dot(lower_ref[...], _one_hot_bf16(mine), preferred_element_type=F32)
    start = jnp.zeros((1, LANES), jnp.int32)
    for g in range(ng):
        start = jnp.where(lane == g, meta_ref[0, 0, g], start)
    dest = jnp.sum(jnp.where(mine, seen + start.astype(F32), 0.0), axis=-1, keepdims=True)
    pick = _one_hot_bf16(lax.broadcasted_iota(jnp.int32, (1, rloc), 1) == dest.astype(jnp.int32))

    _moe_block_moves(meta_ref, copy_into(slot), lambda c: c.wait())
    y_hi, y_lo = _split_bf16(buf[slot])
    y = (jnp.dot(pick, y_hi, preferred_element_type=F32) + jnp.dot(pick, y_lo, preferred_element_type=F32))
    o_ref[0] = x1_ref[0] + mod_ref[0, 0, 5:6, :] * y


def _combine(meta, gates, x1, modtab, lower, ys, i0, nct):
    b, s, d = x1.shape
    tm = lower.shape[0]
    nta = s // tm - i0
    n_meta = meta.shape[0]
    rloc = _moe_local_rows(tm)

    def meta_at(delta):
        return lambda bb, i: (jnp.minimum(bb * nta + i + delta, n_meta - 1), 0, 0)

    row = lambda bb, i: (bb, i + i0, 0)
    return pl.pallas_call(
        _combine_kernel,
        out_shape=jax.ShapeDtypeStruct((b, nta * tm, d), F32),
        grid=(b, nta),
        in_specs=[pl.BlockSpec((1, 1, META_LANES), meta_at(0), memory_space=pltpu.SMEM),
                  pl.BlockSpec((1, 1, META_LANES), meta_at(1), memory_space=pltpu.SMEM),
                  pl.BlockSpec((1, tm, LANES), row),
                  pl.BlockSpec((1, tm, d), row),
                  pl.BlockSpec((1, 1, 6, d), lambda bb, i: (bb, jnp.where(i + i0 >= nct, 1, 0), 0, 0)),
                  pl.BlockSpec((tm, tm), lambda bb, i: (0, 0)),
                  pl.BlockSpec(memory_space=pl.ANY)],
        out_specs=pl.BlockSpec((1, tm, d), lambda bb, i: (bb, i, 0)),
        scratch_shapes=[pltpu.VMEM((2, rloc, d), F32), pltpu.SemaphoreType.DMA((2,))],
        compiler_params=_params("arbitrary", "arbitrary"),
    )(meta, meta, gates, x1, modtab, lower, ys)


def _experts(h2, gates, gates_t, x1, modtab, wg, wu, wd, upper, lower, i0, nct):
    b, s, d = x1.shape
    tm = gates_t.shape[3]
    nta = s // tm - i0
    n_sorted = _moe_sorted_rows(b * nta, tm)
    group_ids = gates[:, i0 * tm:, GROUP_ROW].astype(jnp.int32).reshape(b * nta, tm)
    meta, tile_group, n_used = _moe_plan(group_ids, n_sorted)
    xs = _dispatch(meta, h2, gates, gates_t, upper, n_sorted, i0)
    ys = _group_ffn(tile_group, n_used, xs, wg, wu, wd)
    return _combine(meta, gates, x1, modtab, lower, ys, i0, nct)


def _rope_tables(l, lc):
    t = jnp.arange(l, dtype=jnp.int32)
    row = (t // GRID_W).astype(F32)
    col = (t % GRID_W).astype(F32)
    inv_freq = ROPE_BASE ** (-jnp.arange(ROPE_PAIRS_PER_AXIS, dtype=F32) / ROPE_PAIRS_PER_AXIS)
    ang = jnp.concatenate([row[:, None] * inv_freq, col[:, None] * inv_freq], axis=-1)
    cos, sin = jnp.cos(ang), jnp.sin(ang)
    zero = jnp.zeros_like(sin)
    cos_h = jnp.concatenate([cos, cos], axis=-1)
    sin_lo = jnp.concatenate([-sin, zero], axis=-1)
    sin_hi = jnp.concatenate([zero, sin], axis=-1)
    def full(tab, ctx_value):
        tab = jnp.tile(tab, (1, NA_HEADS))
        return jnp.concatenate([jnp.full((lc, NA_WIDTH), ctx_value, F32), tab], axis=0)
    return full(cos_h, 1.0), full(sin_lo, 0.0), full(sin_hi, 0.0)


def kernel(x, c, ctx, c_ctx, w_ada, b_ada, norm1_g, norm2_g, w_in, q_norm_g, k_norm_g, rpb, s5_lam_re, s5_lam_im, s5_log_step, s5_b_re, s5_b_im, s5_c_re, s5_c_im, s5_d, w_glu, b_glu, w_fnet, w_out, w_router, b_router, w_gate, w_up, w_down):
    b, l, d = x.shape
    lc = ctx.shape[1]
    s = lc + l
    depth = w_ada.shape[0]
    rows = l // GRID_W
    tm = min(256, lc)
    assert lc % tm == 0 and l % tm == 0 and l % GRID_W == 0
    nct = lc // tm
    rb = 4 if rows % 4 == 0 else 1
    tc = min(64, lc)
    nb = 2 if b % 2 == 0 else 1
    ident = jnp.eye(nb * tm, dtype=BF16)

    xa = jnp.concatenate([ctx, x], axis=1)

    n_mod = -(-(b + 1) // SUBLANES) * SUBLANES
    cvec = jnp.concatenate([c, c_ctx[None, :], jnp.zeros((n_mod - b - 1, d), F32)], axis=0)
    mod_all = _modulation(cvec, w_ada, b_ada)

    cos, sin_lo, sin_hi = _rope_tables(l, lc)
    head_ones = jnp.asarray(np.kron(np.eye(NA_HEADS), np.ones((HEAD_DIM, HEAD_DIM))), dtype=BF16)
    csl, csc, c64, s64 = _fnet_constants(l, lc)
    w_router_t = w_router.T
    earlier = np.triu(np.ones((tm, tm)), 1)
    upper = jnp.asarray(earlier, dtype=BF16)
    lower = jnp.asarray(earlier.T, dtype=BF16)
    by_group = lambda w: w.astype(BF16).reshape((N_EXPERT_GROUPS, EXPERTS_PER_GROUP) + w.shape[1:])

    for layer in range(depth):
        mod = mod_all[layer]
        modtab = jnp.stack([jnp.broadcast_to(mod[b].reshape(1, 6, d), (b, 6, d)),
                            mod[:b].reshape(b, 6, d)], axis=1)
        qg = jnp.tile(q_norm_g[layer] * ATTN_SCALE, NA_HEADS).reshape(1, NA_WIDTH)
        kg = jnp.tile(k_norm_g[layer], NA_HEADS).reshape(1, NA_WIDTH)
        q, k, v, us, uf = _in_projection(xa, modtab, norm1_g[layer].reshape(1, d), w_in[layer].astype(BF16),
                                         qg, kg, head_ones, cos, sin_lo, sin_hi, tm, nct, nb)

        bias = _attn_bias_tables(rpb[layer], rows, rb)
        att = _attention(q, k, v, bias, lc, rows, rb)

        bmat, cmat, a_re, a_im = _s5_matrices(s5_lam_re[layer], s5_lam_im[layer], s5_log_step[layer],
                                              s5_b_re[layer], s5_b_im[layer], s5_c_re[layer], s5_c_im[layer], b)
        y_fwd, y_bwd = _s5_scan(us, bmat, cmat, a_re, a_im, lc, tc)

        wbd = jnp.einsum('hde,hg->hdge', w_fnet[layer], jnp.eye(FNET_HEADS, dtype=F32))
        fou = _fourier(uf, csl, csc, c64, s64, wbd.reshape(FNET_WIDTH, FNET_WIDTH).astype(BF16), lc, nb)

        x1, h2, gates, gates_t = _out_projection(
            xa, att, y_fwd, y_bwd, us, fou, modtab, s5_d[layer].reshape(1, S5_WIDTH),
            w_glu[layer].astype(BF16), b_glu[layer].reshape(1, S5_WIDTH), w_out[layer].astype(BF16),
            norm2_g[layer].reshape(1, d), w_router_t, b_router, ident, tm, nct, nb)

        i0 = nct if layer == depth - 1 else 0
        xa = _experts(h2, gates, gates_t, x1, modtab, by_group(w_gate[layer]), by_group(w_up[layer]),
                      by_group(w_down[layer]), upper, lower, i0, nct)

    return xa
```

```python
import functools
import math

import numpy as np
import jax
import jax.numpy as jnp
from jax import lax
from jax.experimental import pallas as pl
from jax.experimental.pallas import tpu as pltpu

F32 = jnp.float32
BF16 = jnp.bfloat16

GRID_W = 64
EPS = 1e-6
HEAD_DIM = 64
NA_WIDTH = 512
NA_HEADS = 8
NA_WIN_H = 8
NA_WIN_W = 16
S5_WIDTH = 256
S5_GROUP = 16
S5_GROUPS = 16
S5_STATE = 64
S5_COLS = S5_GROUPS * S5_STATE
FNET_WIDTH = 256
FNET_HEADS = 4
FNET_HEAD_DIM = 64
IN_COLS = 3 * NA_WIDTH + S5_WIDTH + FNET_WIDTH
ATTN_SCALE = HEAD_DIM ** -0.5
ROPE_BASE = 100.0
ROPE_PAIRS_PER_AXIS = HEAD_DIM // 4
N_EXPERTS = 16
GROUP_ROW = N_EXPERTS
N_EXPERT_GROUPS = 4
EXPERTS_PER_GROUP = 4
EXPERT_FF = 512

LANES = 128
SUBLANES = 8
VMEM_LIMIT = 56 * 1024 * 1024

NEG_BIG = -1e30
NT_DIMS = (((1,), (1,)), ((), ()))


def _split_bf16(x):
    hi = x.astype(BF16)
    lo = (x - hi.astype(F32)).astype(BF16)
    return hi, lo


def _split3_bf16(x):
    p0 = x.astype(BF16)
    r1 = x - p0.astype(F32)
    p1 = r1.astype(BF16)
    p2 = (r1 - p1.astype(F32)).astype(BF16)
    return p0, p1, p2


def _dot_f32(a, b, dims=None):
    a_hi, a_lo = _split_bf16(a)
    b_hi, b_lo = _split_bf16(b)
    if dims is None:
        dot = functools.partial(jnp.dot, preferred_element_type=F32)
    else:
        dot = functools.partial(lax.dot_general, dimension_numbers=dims, preferred_element_type=F32)
    return dot(a_hi, b_hi) + dot(a_hi, b_lo) + dot(a_lo, b_hi)


def _params(*sem):
    return pltpu.CompilerParams(dimension_semantics=sem, vmem_limit_bytes=VMEM_LIMIT)


def _mod_kernel(c_ref, w_ref, b_ref, o_ref):
    c = c_ref[...]
    s = c * jax.nn.sigmoid(c)
    o_ref[0] = _dot_f32(s, w_ref[0]) + b_ref[0]


def _modulation(cvec, w_ada, b_ada):
    depth, d, n = w_ada.shape
    r = cvec.shape[0]
    tn = 768
    return pl.pallas_call(
        _mod_kernel,
        out_shape=jax.ShapeDtypeStruct((depth, r, n), F32),
        grid=(depth, n // tn),
        in_specs=[pl.BlockSpec((r, d), lambda l, j: (0, 0)),
                  pl.BlockSpec((1, d, tn), lambda l, j: (l, 0, j)),
                  pl.BlockSpec((1, 1, tn), lambda l, j: (l, 0, j))],
        out_specs=pl.BlockSpec((1, r, tn), lambda l, j: (l, 0, j)),
        compiler_params=_params("arbitrary", "arbitrary"),
    )(cvec, w_ada, b_ada.reshape(depth, 1, n))


def _inproj_kernel(x_ref, mod_ref, g1_ref, w_ref, qg_ref, kg_ref, gm_ref, cos_ref, sa_ref, sb_ref,
                   q_ref, k_ref, v_ref, us_ref, uf_ref):
    cos = cos_ref[...]
    sin_lo = sa_ref[...]
    sin_hi = sb_ref[...]
    gm = gm_ref[...]
    nw = NA_WIDTH

    def head_norm_rope(t, g):
        ss = jnp.dot((t * t).astype(BF16), gm, preferred_element_type=F32)
        tn = t * lax.rsqrt(ss * (1.0 / HEAD_DIM) + EPS) * g
        half = HEAD_DIM // 2
        return (tn * cos + pltpu.roll(tn, nw - half, 1) * sin_lo + pltpu.roll(tn, half, 1) * sin_hi)

    for n in range(x_ref.shape[0]):
        x = x_ref[n]
        mod = mod_ref[n, 0]
        r = lax.rsqrt(jnp.mean(x * x, axis=-1, keepdims=True) + EPS)
        h = (x * r * g1_ref[...]) * (1.0 + mod[1:2]) + mod[0:1]
        proj = jnp.dot(h.astype(BF16), w_ref[...], preferred_element_type=F32)
        q_ref[n] = head_norm_rope(proj[:, 0:nw], qg_ref[...]).astype(BF16)
        k_ref[n] = head_norm_rope(proj[:, nw:2 * nw], kg_ref[...]).astype(BF16)
        v_ref[n] = proj[:, 2 * nw:3 * nw].astype(BF16)
        us_ref[n] = proj[:, 3 * nw:3 * nw + S5_WIDTH]
        uf_ref[n] = proj[:, 3 * nw + S5_WIDTH:].astype(BF16)


def _in_projection(xa, modtab, g1, w_in_bf, qg, kg, gm, cos, sin_lo, sin_hi, tm, nct, nb):
    b, s, d = xa.shape
    nt = s // tm
    nw = NA_WIDTH
    row = lambda i, bb: (bb, i, 0)
    const2 = lambda i, bb: (0, 0)
    tab = lambda i, bb: (i, 0)
    return pl.pallas_call(
        _inproj_kernel,
        out_shape=(jax.ShapeDtypeStruct((b, s, nw), BF16),
                   jax.ShapeDtypeStruct((b, s, nw), BF16),
                   jax.ShapeDtypeStruct((b, s, nw), BF16),
                   jax.ShapeDtypeStruct((b, s, S5_WIDTH), F32),
                   jax.ShapeDtypeStruct((b, s, FNET_WIDTH), BF16)),
        grid=(nt, b // nb),
        in_specs=[pl.BlockSpec((nb, tm, d), row),
                  pl.BlockSpec((nb, 1, 6, d), lambda i, bb: (bb, jnp.where(i >= nct, 1, 0), 0, 0)),
                  pl.BlockSpec((1, d), const2),
                  pl.BlockSpec((d, IN_COLS), const2),
                  pl.BlockSpec((1, nw), const2),
                  pl.BlockSpec((1, nw), const2),
                  pl.BlockSpec((nw, nw), const2),
                  pl.BlockSpec((tm, nw), tab),
                  pl.BlockSpec((tm, nw), tab),
                  pl.BlockSpec((tm, nw), tab)],
        out_specs=(pl.BlockSpec((nb, tm, nw), row),
                   pl.BlockSpec((nb, tm, nw), row),
                   pl.BlockSpec((nb, tm, nw), row),
                   pl.BlockSpec((nb, tm, S5_WIDTH), row),
                   pl.BlockSpec((nb, tm, FNET_WIDTH), row)),
        compiler_params=_params("arbitrary", "arbitrary"),
    )(xa, modtab, g1, w_in_bf, qg, kg, gm, cos, sin_lo, sin_hi)


def _attn_blocks(rows, rb):
    kh = min(NA_WIN_H, rows)
    kr = min(rb + kh - 1, rows)
    blocks, classes = [], {}
    for r0 in range(0, rows, rb):
        ks = min(max(r0 - kh // 2, 0), rows - kr)
        starts = tuple(min(max(r - kh // 2, 0), rows - kh) - ks for r in range(r0, r0 + rb))
        key = (r0 - ks, starts)
        cls = classes.setdefault(key, len(classes))
        blocks.append((r0, ks, cls))
    return kh, kr, blocks, list(classes.keys())


def _attn_bias_tables(rpb, rows, rb):
    kh, kr, blocks, class_keys = _attn_blocks(rows, rb)
    w = GRID_W
    col = np.arange(w)
    col_start = np.clip(col - NA_WIN_W // 2, 0, w - NA_WIN_W)
    n_row, n_col = 2 * NA_WIN_H - 1, 2 * NA_WIN_W - 1
    col_off = np.clip(col[None, :] - col[:, None] + (NA_WIN_W - 1), 0, n_col - 1)
    col_pick = jnp.asarray(np.eye(n_col)[col_off].reshape(w * w, n_col), dtype=F32)
    col_valid = (col[None, :] >= col_start[:, None]) & (col[None, :] < col_start[:, None] + NA_WIN_W)
    tabs = []
    for q_minus_ks, starts in class_keys:
        rq = np.arange(rb)[:, None]
        i = np.arange(kr)[None, :]
        rs = np.asarray(starts)[:, None]
        row_valid = (i >= rs) & (i < rs + kh)
        row_off = np.clip(i - (q_minus_ks + rq) + (NA_WIN_H - 1), 0, n_row - 1)
        row_pick = jnp.asarray(np.eye(n_row)[row_off].reshape(rb * kr, n_row), dtype=F32)
        bias = jnp.einsum('hyx,ny,mx->hnm', rpb.astype(F32), row_pick, col_pick,
                          precision=lax.Precision.HIGHEST)
        bias = bias.reshape(-1, rb, kr, w, w).transpose(0, 1, 3, 2, 4)
        valid = row_valid[:, None, :, None] & col_valid[None, :, None, :]
        tabs.append(jnp.where(valid[None], bias, NEG_BIG).reshape(-1, rb * w, kr * w))
    return jnp.stack(tabs, axis=0)


def _attn_kernel(q_ref, k_ref, v_ref, bias_ref, o_ref, *, lc, rb, kr, blocks):
    lane = lax.broadcasted_iota(jnp.int32, (1, LANES), 1)
    first_head = lane < HEAD_DIM

    def stack_heads(qb):
        z = jnp.zeros_like(qb)
        return jnp.concatenate([jnp.where(first_head, qb, z), jnp.where(first_head, z, qb)], axis=0)

    def unstack_heads(o, n):
        return jnp.where(first_head, o[:n], o[n:])

    kc = k_ref[0, 0:lc, :]
    vc = v_ref[0, 0:lc, :]

    s = lax.dot_general(stack_heads(q_ref[0, 0:lc, :]), kc, NT_DIMS, preferred_element_type=F32)
    m = jnp.max(s, axis=-1, keepdims=True)
    p = jnp.exp(s - m)
    den = jnp.sum(p, axis=-1, keepdims=True)
    o = jnp.dot(p.astype(BF16), vc, preferred_element_type=F32) / den
    o_ref[0, 0:lc, :] = unstack_heads(o, lc).astype(BF16)

    nq = rb * GRID_W
    nk = kr * GRID_W
    for r0, ks, cls in blocks:
        q0 = lc + r0 * GRID_W
        k0 = lc + ks * GRID_W
        q2 = stack_heads(q_ref[0, q0:q0 + nq, :])
        k_loc = k_ref[0, k0:k0 + nk, :]
        v_loc = v_ref[0, k0:k0 + nk, :]
        s_loc = lax.dot_general(q2, k_loc, NT_DIMS, preferred_element_type=F32)
        s_loc = s_loc + bias_ref[cls].reshape(2 * nq, nk)
        s_ctx = lax.dot_general(q2, kc, NT_DIMS, preferred_element_type=F32)
        m = jnp.maximum(jnp.max(s_loc, axis=-1, keepdims=True), jnp.max(s_ctx, axis=-1, keepdims=True))
        p_loc = jnp.exp(s_loc - m)
        p_ctx = jnp.exp(s_ctx - m)
        den = jnp.sum(p_loc, axis=-1, keepdims=True) + jnp.sum(p_ctx, axis=-1, keepdims=True)
        o = (jnp.dot(p_loc.astype(BF16), v_loc, preferred_element_type=F32)
             + jnp.dot(p_ctx.astype(BF16), vc, preferred_element_type=F32)) / den
        o_ref[0, q0:q0 + nq, :] = unstack_heads(o, nq).astype(BF16)


def _attention(q, k, v, bias, lc, rows, rb):
    b, s, nw = q.shape
    _, kr, blocks, _ = _attn_blocks(rows, rb)
    ncls = bias.shape[0]
    spec = pl.BlockSpec((1, s, LANES), lambda bb, p: (bb, 0, p))
    return pl.pallas_call(
        functools.partial(_attn_kernel, lc=lc, rb=rb, kr=kr, blocks=blocks),
        out_shape=jax.ShapeDtypeStruct((b, s, nw), BF16),
        grid=(b, nw // LANES),
        in_specs=[spec, spec, spec,
                  pl.BlockSpec((ncls, 2, rb * GRID_W, kr * GRID_W), lambda bb, p: (0, p, 0, 0))],
        out_specs=spec,
        compiler_params=_params("arbitrary", "arbitrary"),
    )(q, k, v, bias)


def _s5_matrices(lam_re, lam_im, log_step, b_re, b_im, c_re, c_im, batch):
    lam_re = lam_re.astype(F32)
    lam_im = lam_im.astype(F32)
    step = jnp.exp(log_step.astype(F32))[..., None]
    mag = jnp.exp(lam_re * step)
    a_re = mag * jnp.cos(lam_im * step)
    a_im = mag * jnp.sin(lam_im * step)
    den = lam_re * lam_re + lam_im * lam_im
    co_re = ((a_re - 1.0) * lam_re + a_im * lam_im) / den
    co_im = (a_im * lam_re - (a_re - 1.0) * lam_im) / den
    b_re = b_re.astype(F32)
    b_im = b_im.astype(F32)
    bb_re = co_re[..., None] * b_re - co_im[..., None] * b_im
    bb_im = co_re[..., None] * b_im + co_im[..., None] * b_re
    eye = jnp.eye(S5_GROUPS, dtype=F32)
    to_in = lambda t: jnp.einsum('dgpc,gh->dgchp', t, eye).reshape(2, S5_WIDTH, S5_COLS)
    bmat = jnp.concatenate([to_in(bb_re), to_in(bb_im)], axis=-1)
    to_out = lambda t: jnp.einsum('dgcp,gh->dgphc', t.astype(F32), eye).reshape(2, S5_COLS, S5_WIDTH)
    cmat = jnp.concatenate([to_out(c_re), -to_out(c_im)], axis=1)
    bc = lambda t: jnp.broadcast_to(t.reshape(2, 1, S5_COLS), (2, batch, S5_COLS))
    return bmat.astype(BF16), cmat.astype(BF16), bc(a_re), bc(a_im)


def _s5_direction(u_ref, bm_ref, cm_ref, ar_ref, ai_ref, y_ref, hbuf, st_re, st_im, tc, batch, reverse):
    u = jnp.swapaxes(u_ref[...], 0, 1).reshape(tc * batch, S5_WIDTH)
    hbuf[...] = jnp.dot(u.astype(BF16), bm_ref[0], preferred_element_type=F32)

    ncol = 2 * LANES
    for cg in range(S5_COLS // ncol):
        re = slice(cg * ncol, (cg + 1) * ncol)
        im = slice(S5_COLS + cg * ncol, S5_COLS + (cg + 1) * ncol)
        h_re = st_re[:, re]
        h_im = st_im[:, re]
        for tt in range(tc):
            t = tc - 1 - tt if reverse else tt
            rows = slice(t * batch, (t + 1) * batch)
            a_re = ar_ref[0, :, re]
            a_im = ai_ref[0, :, re]
            n_re = a_re * h_re - a_im * h_im + hbuf[rows, re]
            n_im = a_re * h_im + a_im * h_re + hbuf[rows, im]
            hbuf[rows, re] = n_re
            hbuf[rows, im] = n_im
            h_re, h_im = n_re, n_im
        st_re[:, re] = h_re
        st_im[:, re] = h_im

    y = jnp.dot(hbuf[...].astype(BF16), cm_ref[0], preferred_element_type=F32)
    y_ref[...] = jnp.swapaxes(y.reshape(tc, batch, S5_WIDTH), 0, 1)


def _s5_kernel(uf_ref, ub_ref, bmf_ref, bmb_ref, cmf_ref, cmb_ref, arf_ref, arb_ref, aif_ref, aib_ref,
               yf_ref, yb_ref, hbuf_f, hbuf_b, sf_re, sf_im, sb_re, sb_im, *, tc, batch):
    @pl.when(pl.program_id(0) == 0)
    def _():
        for st in (sf_re, sf_im, sb_re, sb_im):
            st[...] = jnp.zeros_like(st)

    _s5_direction(uf_ref, bmf_ref, cmf_ref, arf_ref, aif_ref, yf_ref, hbuf_f, sf_re, sf_im, tc, batch, False)
    _s5_direction(ub_ref, bmb_ref, cmb_ref, arb_ref, aib_ref, yb_ref, hbuf_b, sb_re, sb_im, tc, batch, True)


def _s5_scan(us, bmat, cmat, a_re, a_im, lc, tc):
    batch, s, w = us.shape
    nch = s // tc
    nch_c = lc // tc
    fwd = lambda j: (0, j, 0)
    bwd = lambda j: (0, jnp.where(j < nch_c, nch_c - 1 - j, nch + nch_c - 1 - j), 0)
    dir_f = lambda j: (0, 0, 0)
    dir_b = lambda j: (1, 0, 0)
    hbuf = pltpu.VMEM((tc * batch, 2 * S5_COLS), F32)
    state = pltpu.VMEM((batch, S5_COLS), F32)
    out = jax.ShapeDtypeStruct((batch, s, w), F32)
    return pl.pallas_call(
        functools.partial(_s5_kernel, tc=tc, batch=batch),
        out_shape=(out, out),
        grid=(nch,),
        in_specs=[pl.BlockSpec((batch, tc, w), fwd),
                  pl.BlockSpec((batch, tc, w), bwd),
                  pl.BlockSpec((1, w, 2 * S5_COLS), dir_f),
                  pl.BlockSpec((1, w, 2 * S5_COLS), dir_b),
                  pl.BlockSpec((1, 2 * S5_COLS, w), dir_f),
                  pl.BlockSpec((1, 2 * S5_COLS, w), dir_b),
                  pl.BlockSpec((1, batch, S5_COLS), dir_f),
                  pl.BlockSpec((1, batch, S5_COLS), dir_b),
                  pl.BlockSpec((1, batch, S5_COLS), dir_f),
                  pl.BlockSpec((1, batch, S5_COLS), dir_b)],
        out_specs=(pl.BlockSpec((batch, tc, w), fwd), pl.BlockSpec((batch, tc, w), bwd)),
        scratch_shapes=[hbuf, hbuf, state, state, state, state],
        compiler_params=_params("arbitrary"),
    )(us, us, bmat, bmat, cmat, cmat, a_re, a_re, a_im, a_im)


def _dft_cos_sin(n):
    k = np.arange(n, dtype=np.int64)
    ang = 2.0 * np.pi * ((k[:, None] * k[None, :]) % n).astype(np.float64) / n
    return np.cos(ang), np.sin(ang)


def _fnet_constants(l, lc):
    def seq_matrix(n):
        c, s = _dft_cos_sin(n)
        scale = 1.0 / math.sqrt(n * FNET_HEAD_DIM)
        return jnp.asarray(np.concatenate([c, -s], axis=1) * scale, dtype=F32).astype(BF16)
    c64, s64 = _dft_cos_sin(FNET_HEAD_DIM)
    eye = np.eye(FNET_HEADS)
    return (seq_matrix(l), seq_matrix(lc),
            jnp.asarray(np.kron(eye, c64), dtype=F32).astype(BF16),
            jnp.asarray(np.kron(eye, s64), dtype=F32).astype(BF16))


def _fnet_kernel(u_ref, csl_ref, csc_ref, c64_ref, s64_ref, w_ref, o_ref, tl_ref, tc_ref, *, lc, l):
    i = pl.program_id(1)
    w = w_ref[...]
    nb, _, width = u_ref.shape

    def finish(z):
        for n in range(nb):
            zn = z[:, n * width:(n + 1) * width].astype(BF16)
            o_ref[n] = jnp.dot(zn, w, preferred_element_type=F32).astype(BF16)

    @pl.when(i == 0)
    def _():
        for n in range(nb):
            u = u_ref[n]
            cols = slice(n * width, (n + 1) * width)
            t_cos = jnp.dot(u, c64_ref[...], preferred_element_type=F32).astype(BF16)
            t_sin = jnp.dot(u, s64_ref[...], preferred_element_type=F32).astype(BF16)
            tc_ref[0:lc, cols] = t_cos[0:lc]
            tc_ref[lc:2 * lc, cols] = t_sin[0:lc]
            tl_ref[0:l, cols] = t_cos[lc:]
            tl_ref[l:2 * l, cols] = t_sin[lc:]
        finish(jnp.dot(csc_ref[...], tc_ref[...], preferred_element_type=F32))

    @pl.when(i > 0)
    def _():
        finish(jnp.dot(csl_ref[...], tl_ref[...], preferred_element_type=F32))


def _fourier(uf, csl, csc, c64, s64, wbd, lc, nb):
    b, s, w = uf.shape
    l = s - lc
    tmf = lc
    nt = 1 + l // tmf
    const = lambda bb, i: (0, 0)
    return pl.pallas_call(
        functools.partial(_fnet_kernel, lc=lc, l=l),
        out_shape=jax.ShapeDtypeStruct((b, s, w), BF16),
        grid=(b // nb, nt),
        in_specs=[pl.BlockSpec((nb, s, w), lambda bb, i: (bb, 0, 0)),
                  pl.BlockSpec((tmf, 2 * l), lambda bb, i: (jnp.maximum(i - 1, 0), 0)),
                  pl.BlockSpec((lc, 2 * lc), const),
                  pl.BlockSpec((w, w), const),
                  pl.BlockSpec((w, w), const),
                  pl.BlockSpec((w, w), const)],
        out_specs=pl.BlockSpec((nb, tmf, w), lambda bb, i: (bb, i, 0)),
        scratch_shapes=[pltpu.VMEM((2 * l, nb * w), BF16), pltpu.VMEM((2 * lc, nb * w), BF16)],
        compiler_params=_params("arbitrary", "arbitrary"),
    )(uf, csl, csc, c64, s64, wbd)


def _gelu_tanh(x):
    return 0.5 * x * (1.0 + jnp.tanh(math.sqrt(2.0 / math.pi) * (x + 0.044715 * (x * x * x))))


def _route(logits_t, b_router_ref):
    e_all = range(N_EXPERTS)
    aff = [jax.nn.sigmoid(logits_t[e:e + 1, :]) for e in e_all]
    sel = [aff[e] + b_router_ref[e] for e in e_all]
    epg = EXPERTS_PER_GROUP
    keep = []
    for e in e_all:
        g0 = (e // epg) * epg
        rank = jnp.zeros_like(sel[e], dtype=jnp.int32)
        for o in range(g0, g0 + epg):
            if o == e:
                continue
            ahead = (sel[o] >= sel[e]) if o < e else (sel[o] > sel[e])
            rank = rank + ahead.astype(jnp.int32)
        keep.append(rank < 2)
    score = []
    for g in range(N_EXPERT_GROUPS):
        s0, s1, s2, s3 = sel[g * epg:(g + 1) * epg]
        hi01, lo01 = jnp.maximum(s0, s1), jnp.minimum(s0, s1)
        hi23, lo23 = jnp.maximum(s2, s3), jnp.minimum(s2, s3)
        top1 = jnp.maximum(hi01, hi23)
        top2 = jnp.maximum(jnp.minimum(hi01, hi23), jnp.maximum(lo01, lo23))
        score.append(top1 + top2)
    chosen = []
    for g in range(N_EXPERT_GROUPS):
        best = None
        for o in range(N_EXPERT_GROUPS):
            if o == g:
                continue
            cond = (score[g] > score[o]) if o < g else (score[g] >= score[o])
            best = cond if best is None else (best & cond)
        chosen.append(best)
    picked = [jnp.where(chosen[e // epg] & keep[e], aff[e], 0.0) for e in e_all]
    total = picked[0]
    for e in range(1, N_EXPERTS):
        total = total + picked[e]
    group = jnp.zeros_like(total)
    for g in range(1, N_EXPERT_GROUPS):
        group = jnp.where(chosen[g], float(g), group)
    return [p / total for p in picked], group


def _outproj_kernel(x_ref, att_ref, yf_ref, yb_ref, us_ref, fou_ref, mod_ref, sd_ref, wglu_ref, bglu_ref,
                    wout_ref, g2_ref, wr_ref, br_ref, ident_ref,
                    x1_ref, h2_ref, gate_ref, gate_t_ref):
    nb, tm = x_ref.shape[0], x_ref.shape[1]
    h2_all = []
    for n in range(nb):
        mod = mod_ref[n, 0]
        y = sd_ref[...] * us_ref[n] + yf_ref[n] + yb_ref[n]
        g = _gelu_tanh(y)
        ssm = g * jax.nn.sigmoid(jnp.dot(g.astype(BF16), wglu_ref[...], preferred_element_type=F32)
                                 + bglu_ref[...])
        a0, a1 = NA_WIDTH, NA_WIDTH + S5_WIDTH
        o = (jnp.dot(att_ref[n], wout_ref[0:a0, :], preferred_element_type=F32)
             + jnp.dot(ssm.astype(BF16), wout_ref[a0:a1, :], preferred_element_type=F32)
             + jnp.dot(fou_ref[n], wout_ref[a1:, :], preferred_element_type=F32))
        x1 = x_ref[n] + mod[2:3] * o
        x1_ref[n] = x1
        r = lax.rsqrt(jnp.mean(x1 * x1, axis=-1, keepdims=True) + EPS)
        h2 = (x1 * r * g2_ref[...]) * (1.0 + mod[4:5]) + mod[3:4]
        h2_ref[n] = h2.astype(BF16)
        h2_all.append(h2)

    h2 = jnp.concatenate(h2_all, axis=0) if nb > 1 else h2_all[0]
    nt = nb * tm
    logits_t = _dot_f32(wr_ref[...], h2, NT_DIMS)
    gates, group = _route(logits_t, br_ref)
    sub = lax.broadcasted_iota(jnp.int32, (LANES, nt), 0)
    gt = jnp.zeros((LANES, nt), F32)
    for e in range(N_EXPERTS):
        gt = jnp.where(sub == e, gates[e], gt)
    gt = jnp.where(sub == GROUP_ROW, group, gt)
    p0, p1, p2 = _split3_bf16(gt)
    ident = ident_ref[...]
    dot_nt = functools.partial(lax.dot_general, dimension_numbers=NT_DIMS, preferred_element_type=F32)
    gate = dot_nt(ident, p0) + dot_nt(ident, p1) + dot_nt(ident, p2)
    for n in range(nb):
        gate_t_ref[n, 0] = gt[:, n * tm:(n + 1) * tm]
        gate_ref[n] = gate[n * tm:(n + 1) * tm, :]


def _out_projection(xa, att, y_fwd, y_bwd, us, fou, modtab, s5_d, w_glu_bf, b_glu, w_out_bf, g2, w_router_t, b_router,
                    ident, tm, nct, nb):
    b, s, d = xa.shape
    nt = s // tm
    row = lambda i, bb: (bb, i, 0)
    const2 = lambda i, bb: (0, 0)
    return pl.pallas_call(
        _outproj_kernel,
        out_shape=(jax.ShapeDtypeStruct((b, s, d), F32),
                   jax.ShapeDtypeStruct((b, s, d), BF16),
                   jax.ShapeDtypeStruct((b, s, LANES), F32),
                   jax.ShapeDtypeStruct((b, nt, LANES, tm), F32)),
        grid=(nt, b // nb),
        in_specs=[pl.BlockSpec((nb, tm, d), row),
                  pl.BlockSpec((nb, tm, NA_WIDTH), row),
                  pl.BlockSpec((nb, tm, S5_WIDTH), row),
                  pl.BlockSpec((nb, tm, S5_WIDTH), row),
                  pl.BlockSpec((nb, tm, S5_WIDTH), row),
                  pl.BlockSpec((nb, tm, FNET_WIDTH), row),
                  pl.BlockSpec((nb, 1, 6, d), lambda i, bb: (bb, jnp.where(i >= nct, 1, 0), 0, 0)),
                  pl.BlockSpec((1, S5_WIDTH), const2),
                  pl.BlockSpec((S5_WIDTH, S5_WIDTH), const2),
                  pl.BlockSpec((1, S5_WIDTH), const2),
                  pl.BlockSpec((d, d), const2),
                  pl.BlockSpec((1, d), const2),
                  pl.BlockSpec((N_EXPERTS, d), const2),
                  pl.BlockSpec(memory_space=pltpu.SMEM),
                  pl.BlockSpec((nb * tm, nb * tm), const2)],
        out_specs=(pl.BlockSpec((nb, tm, d), row),
                   pl.BlockSpec((nb, tm, d), row),
                   pl.BlockSpec((nb, tm, LANES), row),
                   pl.BlockSpec((nb, 1, LANES, tm), lambda i, bb: (bb, i, 0, 0))),
        compiler_params=_params("arbitrary", "arbitrary"),
    )(xa, att, y_fwd, y_bwd, us, fou, modtab, s5_d, w_glu_bf, b_glu, w_out_bf, g2, w_router_t, b_router, ident)


MOE_BLK = SUBLANES
MOE_ROWS = 512
META_LANES = 16


def _moe_local_rows(tm):
    need = tm + N_EXPERT_GROUPS * (MOE_BLK - 1)
    return -(-need // LANES) * LANES


def _moe_sorted_rows(n_tiles, tm):
    need = n_tiles * (tm + N_EXPERT_GROUPS * (MOE_BLK - 1))
    return -(-need // MOE_ROWS) * MOE_ROWS + N_EXPERT_GROUPS * MOE_ROWS


def _moe_plan(group_ids, n_sorted):
    ng = N_EXPERT_GROUPS
    onehot = group_ids[:, :, None] == jnp.arange(ng, dtype=jnp.int32)[None, None, :]
    count = jnp.sum(onehot.astype(jnp.int32), axis=1)
    padded = (count + (MOE_BLK - 1)) // MOE_BLK * MOE_BLK
    local_start = jnp.cumsum(padded, axis=1) - padded
    before = jnp.cumsum(padded, axis=0) - padded
    region = (jnp.sum(padded, axis=0) + (MOE_ROWS - 1)) // MOE_ROWS * MOE_ROWS
    region_end = jnp.cumsum(region)
    offset = (region_end - region)[None, :] + before
    meta = jnp.concatenate([local_start, offset, padded // MOE_BLK, jnp.zeros_like(count)], axis=1)
    tile_row = jnp.arange(n_sorted // MOE_ROWS, dtype=jnp.int32) * MOE_ROWS
    tile_group = jnp.minimum(jnp.sum((tile_row[:, None] >= region_end[None, :]).astype(jnp.int32), axis=1), ng - 1)
    n_used = (region_end[-1] // MOE_ROWS).reshape(1)
    return meta.astype(jnp.int32).reshape(-1, 1, META_LANES), tile_group.astype(jnp.int32), n_used.astype(jnp.int32)


MOE_MOVE_BLOCKS = (8, 2, 1)


def _moe_block_moves(meta_ref, make_copy, action):
    ng = N_EXPERT_GROUPS
    for g in range(ng):
        local0 = meta_ref[0, 0, g]
        sorted0 = meta_ref[0, 0, ng + g]
        left = meta_ref[0, 0, 2 * ng + g]
        for blocks in MOE_MOVE_BLOCKS:
            rows = blocks * MOE_BLK
            count = left // blocks

            def body(k, carry, local0=local0, sorted0=sorted0, rows=rows):
                local = pl.multiple_of(local0 + k * rows, MOE_BLK)
                srt = pl.multiple_of(sorted0 + k * rows, MOE_BLK)
                action(make_copy(local, srt, rows))
                return carry

            lax.fori_loop(0, count, body, 0)
            local0 = local0 + count * rows
            sorted0 = sorted0 + count * rows
            left = left - count * blocks


def _one_hot_bf16(mask):
    return jnp.where(mask, 1.0, 0.0).astype(BF16)


def _dispatch_kernel(meta_ref, prev_ref, prev2_ref, h_ref, gate_ref, gate_t_ref, upper_ref, init_ref, xs_ref,
                     buf, sem, *, rloc):
    del init_ref
    step = pl.program_id(0) * pl.num_programs(1) + pl.program_id(1)
    last = pl.num_programs(0) * pl.num_programs(1) - 1
    slot = step % 2
    ng = N_EXPERT_GROUPS
    tm = h_ref.shape[1]

    def copy_from(slot_):
        return lambda local, srt, rows: pltpu.make_async_copy(
            buf.at[slot_, pl.ds(local, rows), :], xs_ref.at[pl.ds(srt, rows), :], sem.at[slot_])

    @pl.when(step >= 2)
    def _():
        _moe_block_moves(prev2_ref, copy_from(slot), lambda c: c.wait())

    gid = gate_t_ref[0, 0][GROUP_ROW:GROUP_ROW + 1, :]
    sub = lax.broadcasted_iota(jnp.int32, (SUBLANES, tm), 0)
    mine = sub == gid.astype(jnp.int32)
    seen = jnp.dot(_one_hot_bf16(mine), upper_ref[...], preferred_element_type=F32)
    start = jnp.zeros((SUBLANES, tm), jnp.int32)
    for g in range(ng):
        start = jnp.where(sub == g, meta_ref[0, 0, g], start)
    dest = jnp.sum(jnp.where(mine, seen + start.astype(F32), 0.0), axis=0, keepdims=True)
    pick = _one_hot_bf16(lax.broadcasted_iota(jnp.int32, (rloc, tm), 0) == dest.astype(jnp.int32))

    d = h_ref.shape[2]
    buf[slot, :, 0:d] = jnp.dot(pick, h_ref[0], preferred_element_type=F32)
    p0, p1, p2 = _split3_bf16(gate_ref[0])
    dot = functools.partial(jnp.dot, preferred_element_type=F32)
    buf[slot, :, d:] = dot(pick, p0) + dot(pick, p1) + dot(pick, p2)

    _moe_block_moves(meta_ref, copy_from(slot), lambda c: c.start())

    @pl.when(step == last)
    def _():
        _moe_block_moves(meta_ref, copy_from(slot), lambda c: c.wait())

        @pl.when(step >= 1)
        def _():
            _moe_block_moves(prev_ref, copy_from(1 - slot), lambda c: c.wait())


def _dispatch(meta, h2, gates, gates_t, upper, n_sorted, i0):
    b, s, d = h2.shape
    tm = gates_t.shape[3]
    nta = s // tm - i0
    rloc = _moe_local_rows(tm)
    width = d + LANES

    def meta_at(delta):
        return lambda bb, i: (jnp.maximum(bb * nta + i - delta, 0), 0, 0)

    row = lambda bb, i: (bb, i + i0, 0)
    return pl.pallas_call(
        functools.partial(_dispatch_kernel, rloc=rloc),
        out_shape=jax.ShapeDtypeStruct((n_sorted, width), F32),
        grid=(b, nta),
        in_specs=[pl.BlockSpec((1, 1, META_LANES), meta_at(0), memory_space=pltpu.SMEM),
                  pl.BlockSpec((1, 1, META_LANES), meta_at(1), memory_space=pltpu.SMEM),
                  pl.BlockSpec((1, 1, META_LANES), meta_at(2), memory_space=pltpu.SMEM),
                  pl.BlockSpec((1, tm, d), row),
                  pl.BlockSpec((1, tm, LANES), row),
                  pl.BlockSpec((1, 1, LANES, tm), lambda bb, i: (bb, i + i0, 0, 0)),
                  pl.BlockSpec((tm, tm), lambda bb, i: (0, 0)),
                  pl.BlockSpec(memory_space=pl.ANY)],
        out_specs=pl.BlockSpec(memory_space=pl.ANY),
        scratch_shapes=[pltpu.VMEM((2, rloc, width), F32), pltpu.SemaphoreType.DMA((2,))],
        input_output_aliases={7: 0},
        compiler_params=_params("arbitrary", "arbitrary"),
    )(meta, meta, meta, h2, gates, gates_t, upper, jnp.zeros((n_sorted, width), F32))


def _group_ffn_kernel(tile_group_ref, n_used_ref, xs_ref, wg_ref, wu_ref, wd_ref, o_ref):
    i = pl.program_id(0)
    d = o_ref.shape[1]

    @pl.when(i < n_used_ref[0])
    def _():
        x = xs_ref[:, 0:d].astype(BF16)
        gates = xs_ref[:, d:]
        lane = lax.broadcasted_iota(jnp.int32, (1, LANES), 1)
        first = tile_group_ref[i] * EXPERTS_PER_GROUP
        acc = None
        for k in range(EXPERTS_PER_GROUP):
            ge = jnp.sum(jnp.where(lane == first + k, gates, 0.0), axis=-1, keepdims=True)
            gp = jnp.dot(x, wg_ref[0, k], preferred_element_type=F32)
            up = jnp.dot(x, wu_ref[0, k], preferred_element_type=F32)
            he = (gp * jax.nn.sigmoid(gp)) * up * ge
            part = jnp.dot(he.astype(BF16), wd_ref[0, k], preferred_element_type=F32)
            acc = part if acc is None else acc + part
        o_ref[...] = acc

    @pl.when(i >= n_used_ref[0])
    def _():
        o_ref[...] = jnp.zeros_like(o_ref)


def _group_ffn(tile_group, n_used, xs, wg, wu, wd):
    n_sorted, width = xs.shape
    d = width - LANES
    ff = wg.shape[-1]
    epg = EXPERTS_PER_GROUP
    per_group = lambda i, tg, nu: (tg[i], 0, 0, 0)
    rows = lambda i, tg, nu: (i, 0)
    grid_spec = pltpu.PrefetchScalarGridSpec(
        num_scalar_prefetch=2,
        grid=(n_sorted // MOE_ROWS,),
        in_specs=[pl.BlockSpec((MOE_ROWS, width), rows),
                  pl.BlockSpec((1, epg, d, ff), per_group),
                  pl.BlockSpec((1, epg, d, ff), per_group),
                  pl.BlockSpec((1, epg, ff, d), per_group)],
        out_specs=pl.BlockSpec((MOE_ROWS, d), rows))
    return pl.pallas_call(
        _group_ffn_kernel,
        out_shape=jax.ShapeDtypeStruct((n_sorted, d), F32),
        grid_spec=grid_spec,
        compiler_params=_params("arbitrary"),
    )(tile_group, n_used, xs, wg, wu, wd)


def _combine_kernel(meta_ref, next_ref, gate_ref, x1_ref, mod_ref, lower_ref, ys_ref, o_ref, buf, sem):
    step = pl.program_id(0) * pl.num_programs(1) + pl.program_id(1)
    last = pl.num_programs(0) * pl.num_programs(1) - 1
    slot = step % 2
    ng = N_EXPERT_GROUPS
    rloc = buf.shape[1]

    def copy_into(slot_):
        return lambda local, srt, rows: pltpu.make_async_copy(
            ys_ref.at[pl.ds(srt, rows), :], buf.at[slot_, pl.ds(local, rows), :], sem.at[slot_])

    @pl.when(step == 0)
    def _():
        buf[...] = jnp.zeros_like(buf)
        _moe_block_moves(meta_ref, copy_into(0), lambda c: c.start())

    @pl.when(step < last)
    def _():
        _moe_block_moves(next_ref, copy_into(1 - slot), lambda c: c.start())

    gates = gate_ref[0]
    gid = gates[:, GROUP_ROW:GROUP_ROW + 1].astype(jnp.int32)
    lane = lax.broadcasted_iota(jnp.int32, (1, LANES), 1)
    mine = lane == gid
    seen = jnp.dot(lower_ref[...], _one_hot_bf16(mine), preferred_element_type=F32)
    start = jnp.zeros((1, LANES), jnp.int32)
    for g in range(ng):
        start = jnp.where(lane == g, meta_ref[0, 0, g], start)
    dest = jnp.sum(jnp.where(mine, seen + start.astype(F32), 0.0), axis=-1, keepdims=True)
    pick = _one_hot_bf16(lax.broadcasted_iota(jnp.int32, (1, rloc), 1) == dest.astype(jnp.int32))

    _moe_block_moves(meta_ref, copy_into(slot), lambda c: c.wait())
    y_hi, y_lo = _split_bf16(buf[slot])
    y = (jnp.dot(pick, y_hi, preferred_element_type=F32) + jnp.dot(pick, y_lo, preferred_element_type=F32))
    o_ref[0] = x1_ref[0] + mod_ref[0, 0, 5:6, :] * y


def _combine(meta, gates, x1, modtab, lower, ys, i0, nct):
    b, s, d = x1.shape
    tm = lower.shape[0]
    nta = s // tm - i0
    n_meta = meta.shape[0]
    rloc = _moe_local_rows(tm)

    def meta_at(delta):
        return lambda bb, i: (jnp.minimum(bb * nta + i + delta, n_meta - 1), 0, 0)

    row = lambda bb, i: (bb, i + i0, 0)
    return pl.pallas_call(
        _combine_kernel,
        out_shape=jax.ShapeDtypeStruct((b, nta * tm, d), F32),
        grid=(b, nta),
        in_specs=[pl.BlockSpec((1, 1, META_LANES), meta_at(0), memory_space=pltpu.SMEM),
                  pl.BlockSpec((1, 1, META_LANES), meta_at(1), memory_space=pltpu.SMEM),
                  pl.BlockSpec((1, tm, LANES), row),
                  pl.BlockSpec((1, tm, d), row),
                  pl.BlockSpec((1, 1, 6, d), lambda bb, i: (bb, jnp.where(i + i0 >= nct, 1, 0), 0, 0)),
                  pl.BlockSpec((tm, tm), lambda bb, i: (0, 0)),
                  pl.BlockSpec(memory_space=pl.ANY)],
        out_specs=pl.BlockSpec((1, tm, d), lambda bb, i: (bb, i, 0)),
        scratch_shapes=[pltpu.VMEM((2, rloc, d), F32), pltpu.SemaphoreType.DMA((2,))],
        compiler_params=_params("arbitrary", "arbitrary"),
    )(meta, meta, gates, x1, modtab, lower, ys)


def _experts(h2, gates, gates_t, x1, modtab, wg, wu, wd, upper, lower, i0, nct):
    b, s, d = x1.shape
    tm = gates_t.shape[3]
    nta = s // tm - i0
    n_sorted = _moe_sorted_rows(b * nta, tm)
    group_ids = gates[:, i0 * tm:, GROUP_ROW].astype(jnp.int32).reshape(b * nta, tm)
    meta, tile_group, n_used = _moe_plan(group_ids, n_sorted)
    xs = _dispatch(meta, h2, gates, gates_t, upper, n_sorted, i0)
    ys = _group_ffn(tile_group, n_used, xs, wg, wu, wd)
    return _combine(meta, gates, x1, modtab, lower, ys, i0, nct)


def _rope_tables(l, lc):
    t = jnp.arange(l, dtype=jnp.int32)
    row = (t // GRID_W).astype(F32)
    col = (t % GRID_W).astype(F32)
    inv_freq = ROPE_BASE ** (-jnp.arange(ROPE_PAIRS_PER_AXIS, dtype=F32) / ROPE_PAIRS_PER_AXIS)
    ang = jnp.concatenate([row[:, None] * inv_freq, col[:, None] * inv_freq], axis=-1)
    cos, sin = jnp.cos(ang), jnp.sin(ang)
    zero = jnp.zeros_like(sin)
    cos_h = jnp.concatenate([cos, cos], axis=-1)
    sin_lo = jnp.concatenate([-sin, zero], axis=-1)
    sin_hi = jnp.concatenate([zero, sin], axis=-1)
    def full(tab, ctx_value):
        tab = jnp.tile(tab, (1, NA_HEADS))
        return jnp.concatenate([jnp.full((lc, NA_WIDTH), ctx_value, F32), tab], axis=0)
    return full(cos_h, 1.0), full(sin_lo, 0.0), full(sin_hi, 0.0)


def kernel(x, c, ctx, c_ctx, w_ada, b_ada, norm1_g, norm2_g, w_in, q_norm_g, k_norm_g, rpb, s5_lam_re, s5_lam_im, s5_log_step, s5_b_re, s5_b_im, s5_c_re, s5_c_im, s5_d, w_glu, b_glu, w_fnet, w_out, w_router, b_router, w_gate, w_up, w_down):
    b, l, d = x.shape
    lc = ctx.shape[1]
    s = lc + l
    depth = w_ada.shape[0]
    rows = l // GRID_W
    tm = min(256, lc)
    assert lc % tm == 0 and l % tm == 0 and l % GRID_W == 0
    nct = lc // tm
    rb = 4 if rows % 4 == 0 else 1
    tc = min(64, lc)
    nb = 2 if b % 2 == 0 else 1
    ident = jnp.eye(nb * tm, dtype=BF16)

    xa = jnp.concatenate([ctx, x], axis=1)

    n_mod = -(-(b + 1) // SUBLANES) * SUBLANES
    cvec = jnp.concatenate([c, c_ctx[None, :], jnp.zeros((n_mod - b - 1, d), F32)], axis=0)
    mod_all = _modulation(cvec, w_ada, b_ada)

    cos, sin_lo, sin_hi = _rope_tables(l, lc)
    head_ones = jnp.asarray(np.kron(np.eye(NA_HEADS), np.ones((HEAD_DIM, HEAD_DIM))), dtype=BF16)
    csl, csc, c64, s64 = _fnet_constants(l, lc)
    w_router_t = w_router.T
    earlier = np.triu(np.ones((tm, tm)), 1)
    upper = jnp.asarray(earlier, dtype=BF16)
    lower = jnp.asarray(earlier.T, dtype=BF16)
    by_group = lambda w: w.astype(BF16).reshape((N_EXPERT_GROUPS, EXPERTS_PER_GROUP) + w.shape[1:])

    for layer in range(depth):
        mod = mod_all[layer]
        modtab = jnp.stack([jnp.broadcast_to(mod[b].reshape(1, 6, d), (b, 6, d)),
                            mod[:b].reshape(b, 6, d)], axis=1)
        qg = jnp.tile(q_norm_g[layer] * ATTN_SCALE, NA_HEADS).reshape(1, NA_WIDTH)
        kg = jnp.tile(k_norm_g[layer], NA_HEADS).reshape(1, NA_WIDTH)
        q, k, v, us, uf = _in_projection(xa, modtab, norm1_g[layer].reshape(1, d), w_in[layer].astype(BF16),
                                         qg, kg, head_ones, cos, sin_lo, sin_hi, tm, nct, nb)

        bias = _attn_bias_tables(rpb[layer], rows, rb)
        att = _attention(q, k, v, bias, lc, rows, rb)

        bmat, cmat, a_re, a_im = _s5_matrices(s5_lam_re[layer], s5_lam_im[layer], s5_log_step[layer],
                                              s5_b_re[layer], s5_b_im[layer], s5_c_re[layer], s5_c_im[layer], b)
        y_fwd, y_bwd = _s5_scan(us, bmat, cmat, a_re, a_im, lc, tc)

        wbd = jnp.einsum('hde,hg->hdge', w_fnet[layer], jnp.eye(FNET_HEADS, dtype=F32))
        fou = _fourier(uf, csl, csc, c64, s64, wbd.reshape(FNET_WIDTH, FNET_WIDTH).astype(BF16), lc, nb)

        x1, h2, gates, gates_t = _out_projection(
            xa, att, y_fwd, y_bwd, us, fou, modtab, s5_d[layer].reshape(1, S5_WIDTH),
            w_glu[layer].astype(BF16), b_glu[layer].reshape(1, S5_WIDTH), w_out[layer].astype(BF16),
            norm2_g[layer].reshape(1, d), w_router_t, b_router, ident, tm, nct, nb)

        i0 = nct if layer == depth - 1 else 0
        xa = _experts(h2, gates, gates_t, x1, modtab, by_group(w_gate[layer]), by_group(w_up[layer]),
                      by_group(w_down[layer]), upper, lower, i0, nct)

    return xa
```
